```python
import jax
import jax.numpy as jnp
from jax import lax
import numpy as np

D_MODEL = 2048
BATCH = 8
SEQ = 2048
DEPTH = 1

GRID_W = 64
CTX_LEN = 256

D_CONV = D_MODEL // 2
CONV_WIDTH = 31
D_RWKV = D_MODEL // 2
HEAD_SIZE = 64
N_HEADS_RWKV = D_RWKV // HEAD_SIZE
D_DECAY_LORA = 64
D_AAA_LORA = 64
D_GATE_LORA = 160
N_DIR = 2
GN_EPS = 64e-5
D_SHIFT = 3 * D_RWKV + N_DIR * D_DECAY_LORA + N_DIR * D_AAA_LORA + D_GATE_LORA
P_IN = 2 * D_CONV + D_SHIFT + 2 * D_MODEL
RWKV_SPLITS = (D_RWKV, 2 * D_RWKV, 3 * D_RWKV, 3 * D_RWKV + N_DIR * D_DECAY_LORA,
               3 * D_RWKV + N_DIR * (D_DECAY_LORA + D_AAA_LORA))
N_EXPERTS = 32
TOP_K = 4
D_EXPERT = D_MODEL
SWIGLU_LIMIT = 7.0
SWIGLU_ALPHA = 1.702
LN_EPS = 1e-5
DEEPNORM_ALPHA = (2.0 * DEPTH) ** 0.25
DEEPNORM_BETA = (8.0 * DEPTH) ** -0.25

kernel_name = 'hybrid_conformer_rwkv7_moe_dit_block'


def _layer_norm(h):
    hf = h.astype(jnp.float32)
    mu = jnp.mean(hf, -1, keepdims=True)
    var = jnp.mean(jnp.square(hf - mu), -1, keepdims=True)
    return ((hf - mu) * lax.rsqrt(var + LN_EPS)).astype(h.dtype)


def _modulate(h, shift, scale):
    return _layer_norm(h) * (1.0 + scale[..., None, :]) + shift[..., None, :]


def _post_norm(res, out, gate, g, b):
    return _layer_norm(DEEPNORM_ALPHA * res + gate[..., None, :] * out) * g + b


def _centred_shift(z, mu_prev, mu_next):
    zero = jnp.zeros_like(z[:, :1])
    prev = jnp.concatenate([zero, z[:, :-1]], axis=1)
    nxt = jnp.concatenate([z[:, 1:], zero], axis=1)
    return z + mu_prev * (prev - z) + mu_next * (nxt - z)


def _conv_branch(zc, p, rows):
    B, T, _ = zc.shape
    h = zc[..., :D_CONV] * jax.nn.sigmoid(zc[..., D_CONV:])
    if rows is not None:
        h = h.reshape(B * rows, GRID_W, D_CONV)
    h = lax.conv_general_dilated(h, p['conv_w'][:, None, :], window_strides=(1,),
                                 padding=[(CONV_WIDTH // 2, CONV_WIDTH // 2)],
                                 dimension_numbers=('NWC', 'WIO', 'NWC'),
                                 feature_group_count=D_CONV)
    h = h.reshape(B, T, D_CONV) + p['conv_b']
    h = _layer_norm(h) * p['conv_ln_g'] + p['conv_ln_b']
    return jax.nn.silu(h) @ p['w_conv_o'] + p['b_conv_o']


def _rwkv_prepare(zr, p):
    B, T, _ = zr.shape
    zr = _centred_shift(zr, p['shift_mu'][0], p['shift_mu'][1])
    r, k, v, wd, ad, gd = jnp.split(zr, RWKV_SPLITS, axis=-1)
    heads = lambda t: t.reshape(B, T, N_HEADS_RWKV, HEAD_SIZE)
    g = jax.nn.sigmoid(gd) @ p['g2']
    kk = heads(k * p['k_k']).astype(jnp.float32)
    kk = kk / jnp.maximum(jnp.sqrt(jnp.sum(kk * kk, -1, keepdims=True)), 1e-12)
    dirs = []
    for d in range(N_DIR):
        wd_d = wd[..., d * D_DECAY_LORA:(d + 1) * D_DECAY_LORA]
        ad_d = ad[..., d * D_AAA_LORA:(d + 1) * D_AAA_LORA]
        w_log = -jax.nn.softplus(-(p['w0'][d] + jnp.tanh(wd_d) @ p['w2'][d])) - 0.5
        decay = jnp.exp(-jnp.exp(w_log.astype(jnp.float32)))
        a = jax.nn.sigmoid(p['a0'][d] + ad_d @ p['a2'][d])
        k_d = k * (1.0 + (a - 1.0) * p['k_a'])
        dirs.append((heads(decay), heads(k_d), heads(a).astype(jnp.float32)))
    return heads(r), heads(v), kk, g, dirs


def _wkv_scan(state0, r, decay, k, v, a_vec, b_vec, reverse):
    def step(S, inp):
        r_t, w_t, k_t, v_t, a_t, b_t = inp
        sa = jnp.einsum('bhvk,bhk->bhv', S, a_t)
        S = S * w_t[:, :, None, :] + sa[..., None] * b_t[:, :, None, :] + v_t[..., None] * k_t[:, :, None, :]
        return S, jnp.einsum('bhvk,bhk->bhv', S, r_t)
    xs = tuple(jnp.moveaxis(t.astype(jnp.float32), 1, 0) for t in (r, decay, k, v, a_vec, b_vec))
    S, ys = lax.scan(step, state0, xs, reverse=reverse)
    return S, jnp.moveaxis(ys, 0, 1)


def _rwkv_out(y, r, v, dirs, g, p):
    B, T = y.shape[:2]
    mu = jnp.mean(y, -1, keepdims=True)
    var = jnp.mean(jnp.square(y - mu), -1, keepdims=True)
    yn = ((y - mu) * lax.rsqrt(var + GN_EPS)).reshape(B, T, D_RWKV).astype(r.dtype)
    yn = yn * p['lnx_g'] + p['lnx_b']
    rk = sum(jnp.sum(r * k_d * p['r_k'], -1, keepdims=True) for (_, k_d, _) in dirs)
    bonus = (rk * v).reshape(B, T, D_RWKV)
    return ((yn + bonus) * g) @ p['w_rwkv_o']


def _merge(y_conv, y_rwkv, zg, p):
    mix = jax.nn.sigmoid(zg[..., :D_MODEL]) * y_conv + jax.nn.sigmoid(zg[..., D_MODEL:]) * y_rwkv
    return mix @ p['w_out'] + p['b_out']


def _split_proj(z):
    return z[..., :2 * D_CONV], z[..., 2 * D_CONV:2 * D_CONV + D_SHIFT], z[..., 2 * D_CONV + D_SHIFT:]


def _moe(u, p):
    B, T, D = u.shape
    t = u.reshape(B * T, D)
    logits = (t @ p['w_router'] + p['b_router']).astype(jnp.float32)
    top_vals, top_idx = lax.top_k(logits, TOP_K)
    gates = jax.nn.softmax(top_vals, axis=-1)
    combine = jnp.sum(jax.nn.one_hot(top_idx, N_EXPERTS, dtype=jnp.float32) * gates[..., None], axis=1).astype(t.dtype)
    out = jnp.zeros_like(t)
    for e in range(N_EXPERTS):
        h = t @ p['w_gate_up'][e] + p['b_gate_up'][e]
        gate = jnp.minimum(h[:, :D_EXPERT], SWIGLU_LIMIT)
        up = jnp.clip(h[:, D_EXPERT:], -SWIGLU_LIMIT, SWIGLU_LIMIT)
        hid = (up + 1.0) * (gate * jax.nn.sigmoid(SWIGLU_ALPHA * gate))
        out = out + combine[:, e:e + 1] * (hid @ p['w_down'][e] + p['b_down'][e])
    return out.reshape(B, T, D)


def setup_inputs(seed: int = 0) -> dict:
    key = jax.random.key(seed)
    ks = iter(jax.random.split(key, 48))
    nrm = lambda shape, scale: jax.random.normal(next(ks), shape, jnp.float32) * scale
    L = DEPTH
    D = D_MODEL
    w0_base = jnp.linspace(-6.5, -1.5, D_RWKV, dtype=jnp.float32)
    return {
        'x': nrm((BATCH, SEQ, D), 1.0),
        'c': nrm((BATCH, D), 1.0),
        'ctx': nrm((BATCH, CTX_LEN, D), 1.0),
        'c_ctx': nrm((D,), 1.0),
        'w_ada': nrm((L, D, 6 * D), D ** -0.5),
        'b_ada': nrm((L, 6 * D), 0.01),
        'w_in': nrm((L, D, P_IN), D ** -0.5),
        'b_in': nrm((L, P_IN), 0.01),
        'shift_mu': jax.random.uniform(next(ks), (L, 2, D_SHIFT), jnp.float32, 0.0, 0.5),
        'conv_w': nrm((L, CONV_WIDTH, D_CONV), CONV_WIDTH ** -0.5),
        'conv_b': nrm((L, D_CONV), 0.01),
        'conv_ln_g': 1.0 + nrm((L, D_CONV), 0.02),
        'conv_ln_b': nrm((L, D_CONV), 0.01),
        'w_conv_o': nrm((L, D_CONV, D), D_CONV ** -0.5 * DEEPNORM_BETA),
        'b_conv_o': nrm((L, D), 0.01),
        'w0': w0_base + nrm((L, N_DIR, D_RWKV), 0.1),
        'w2': nrm((L, N_DIR, D_DECAY_LORA, D_RWKV), 0.1 * D_DECAY_LORA ** -0.5),
        'a0': nrm((L, N_DIR, D_RWKV), 0.1),
        'a2': nrm((L, N_DIR, D_AAA_LORA, D_RWKV), D_AAA_LORA ** -0.5),
        'g2': nrm((L, D_GATE_LORA, D_RWKV), D_GATE_LORA ** -0.5),
        'k_k': 0.85 + nrm((L, D_RWKV), 0.02),
        'k_a': 1.0 + nrm((L, D_RWKV), 0.02),
        'r_k': nrm((L, N_HEADS_RWKV, HEAD_SIZE), 0.05),
        'lnx_g': 1.0 + nrm((L, D_RWKV), 0.02),
        'lnx_b': nrm((L, D_RWKV), 0.01),
        'w_rwkv_o': nrm((L, D_RWKV, D), D_RWKV ** -0.5 * DEEPNORM_BETA),
        'w_out': nrm((L, D, D), D ** -0.5 * DEEPNORM_BETA),
        'b_out': nrm((L, D), 0.01),
        'ln1_g': 1.0 + nrm((L, D), 0.02),
        'ln1_b': nrm((L, D), 0.01),
        'w_router': nrm((L, D, N_EXPERTS), D ** -0.5),
        'b_router': nrm((L, N_EXPERTS), 0.01),
        'w_gate_up': nrm((L, N_EXPERTS, D, 2 * D_EXPERT), D ** -0.5),
        'b_gate_up': nrm((L, N_EXPERTS, 2 * D_EXPERT), 0.01),
        'w_down': nrm((L, N_EXPERTS, D_EXPERT, D), D_EXPERT ** -0.5 * DEEPNORM_BETA),
        'b_down': nrm((L, N_EXPERTS, D), 0.01),
        'ln2_g': 1.0 + nrm((L, D), 0.02),
        'ln2_b': nrm((L, D), 0.01),
    }


def reference(x, c, ctx, c_ctx, w_ada, b_ada, w_in, b_in, shift_mu, conv_w, conv_b, conv_ln_g, conv_ln_b,
              w_conv_o, b_conv_o, w0, w2, a0, a2, g2, k_k, k_a, r_k, lnx_g, lnx_b, w_rwkv_o, w_out, b_out,
              ln1_g, ln1_b, w_router, b_router, w_gate_up, b_gate_up, w_down, b_down, ln2_g, ln2_b):
    B, n_lat, _ = x.shape
    rows = n_lat // GRID_W
    h_ctx = ctx
    for l in range(DEPTH):
        need_ctx = l < DEPTH - 1
        p = {'shift_mu': shift_mu[l], 'conv_w': conv_w[l], 'conv_b': conv_b[l], 'conv_ln_g': conv_ln_g[l],
             'conv_ln_b': conv_ln_b[l], 'w_conv_o': w_conv_o[l], 'b_conv_o': b_conv_o[l], 'w0': w0[l],
             'w2': w2[l], 'a0': a0[l], 'a2': a2[l], 'g2': g2[l], 'k_k': k_k[l], 'k_a': k_a[l], 'r_k': r_k[l],
             'lnx_g': lnx_g[l], 'lnx_b': lnx_b[l], 'w_rwkv_o': w_rwkv_o[l], 'w_out': w_out[l], 'b_out': b_out[l],
             'w_router': w_router[l], 'b_router': b_router[l], 'w_gate_up': w_gate_up[l],
             'b_gate_up': b_gate_up[l], 'w_down': w_down[l], 'b_down': b_down[l]}
        mod_l = jnp.split(jax.nn.silu(c) @ w_ada[l] + b_ada[l], 6, axis=-1)
        mod_c = jnp.split(jax.nn.silu(c_ctx) @ w_ada[l] + b_ada[l], 6, axis=-1)

        z_l = _modulate(x, mod_l[0], mod_l[1]) @ w_in[l] + b_in[l]
        z_c = _modulate(h_ctx, mod_c[0], mod_c[1]) @ w_in[l] + b_in[l]
        zc_l, zr_l, zg_l = _split_proj(z_l)
        zc_c, zr_c, zg_c = _split_proj(z_c)

        r_c, v_c, kk_c, g_c, dirs_c = _rwkv_prepare(zr_c, p)
        r_l, v_l, kk_l, g_l, dirs_l = _rwkv_prepare(zr_l, p)
        state0 = jnp.zeros((B, N_HEADS_RWKV, HEAD_SIZE, HEAD_SIZE), jnp.float32)
        y_l = 0.0
        y_c = 0.0
        for d in range(N_DIR):
            rev = d == 1
            dec_c, kd_c, a_c = dirs_c[d]
            s_ctx, yc_d = _wkv_scan(state0, r_c, dec_c, kd_c, v_c, -kk_c, kk_c * a_c, rev)
            dec_l, kd_l, a_l = dirs_l[d]
            _, yl_d = _wkv_scan(s_ctx, r_l, dec_l, kd_l, v_l, -kk_l, kk_l * a_l, rev)
            y_l = y_l + yl_d
            y_c = y_c + yc_d
        out_l = _merge(_conv_branch(zc_l, p, rows), _rwkv_out(y_l, r_l, v_l, dirs_l, g_l, p), zg_l, p)
        x_mid = _post_norm(x, out_l, mod_l[2], ln1_g[l], ln1_b[l])

        x = _post_norm(x_mid, _moe(_modulate(x_mid, mod_l[3], mod_l[4]), p), mod_l[5], ln2_g[l], ln2_b[l])

        if need_ctx:
            out_c = _merge(_conv_branch(zc_c, p, None), _rwkv_out(y_c, r_c, v_c, dirs_c, g_c, p), zg_c, p)
            c_mid = _post_norm(h_ctx, out_c, mod_c[2], ln1_g[l], ln1_b[l])
            h_ctx = _post_norm(c_mid, _moe(_modulate(c_mid, mod_c[3], mod_c[4]), p), mod_c[5], ln2_g[l], ln2_b[l])
    return x
```

```python
import functools

import jax
import jax.numpy as jnp
from jax import lax
from jax.experimental import pallas as pl
from jax.experimental.pallas import tpu as pltpu

F32 = jnp.float32
BF16 = jnp.bfloat16
HIGHEST = lax.Precision.HIGHEST

GRID_W = 64
CONV_WIDTH = 31
HEAD_SIZE = 64
D_LORA = 64
D_GATE_LORA = 160
N_DIR = 2
GN_EPS = 64e-5
LN_EPS = 1e-5
TOP_K = 4
SWIGLU_LIMIT = 7.0
SWIGLU_ALPHA = 1.702
DEPTH = 1
DEEPNORM_ALPHA = (2.0 * DEPTH) ** 0.25

LANES = 128
ZR_PAD = 4096
VMEM_LIMIT = 56 * 1024 * 1024
EXPERT_ROW_TILE = 512


def _cparams(sem):
    return pltpu.CompilerParams(dimension_semantics=sem, vmem_limit_bytes=VMEM_LIMIT)


def _ln(x):
    mu = jnp.mean(x, axis=-1, keepdims=True)
    xc = x - mu
    var = jnp.mean(xc * xc, axis=-1, keepdims=True)
    return xc * lax.rsqrt(var + LN_EPS)


def _sigmoid(x):
    return 1.0 / (1.0 + jnp.exp(-x))


def _ada_kernel(c_ref, w_ref, b_ref, o_ref):
    c = c_ref[...]
    s = c * _sigmoid(c)
    o_ref[...] = jnp.dot(s, w_ref[...], preferred_element_type=F32, precision=HIGHEST) + b_ref[...]


def _ada(c_rows, w, b):
    rows, d = c_rows.shape
    n = w.shape[1]
    tn = 1024
    return pl.pallas_call(
        _ada_kernel,
        grid=(n // tn,),
        in_specs=[pl.BlockSpec((rows, d), lambda j: (0, 0)),
                  pl.BlockSpec((d, tn), lambda j: (0, j)),
                  pl.BlockSpec((1, tn), lambda j: (0, j))],
        out_specs=pl.BlockSpec((rows, tn), lambda j: (0, j)),
        out_shape=jax.ShapeDtypeStruct((rows, n), F32),
        compiler_params=_cparams(("arbitrary",)),
        name="ada",
    )(c_rows, w, b.reshape(1, n))


def _lnmod_mm_kernel(x_ref, sh_ref, sc_ref, w_ref, b_ref, o_ref, xn_ref):
    @pl.when(pl.program_id(1) == 0)
    def _():
        xn = _ln(x_ref[...]) * (1.0 + sc_ref[0]) + sh_ref[0]
        xn_ref[...] = xn.astype(BF16)

    o_ref[...] = jnp.dot(xn_ref[...], w_ref[...], preferred_element_type=F32) + b_ref[...]


def _lnmod_mm(x2, mod3, row_of_tile, shift_slot, w_bf16, bias, tm, tn):
    m_rows, d = x2.shape
    n = w_bf16.shape[1]
    return pl.pallas_call(
        _lnmod_mm_kernel,
        grid=(m_rows // tm, n // tn),
        in_specs=[pl.BlockSpec((tm, d), lambda m, j: (m, 0)),
                  pl.BlockSpec((1, 1, d), lambda m, j: (row_of_tile(m) * 6 + shift_slot, 0, 0)),
                  pl.BlockSpec((1, 1, d), lambda m, j: (row_of_tile(m) * 6 + shift_slot + 1, 0, 0)),
                  pl.BlockSpec((d, tn), lambda m, j: (0, j)),
                  pl.BlockSpec((1, tn), lambda m, j: (0, j))],
        out_specs=pl.BlockSpec((tm, tn), lambda m, j: (m, j)),
        out_shape=jax.ShapeDtypeStruct((m_rows, n), F32),
        scratch_shapes=[pltpu.VMEM((tm, d), BF16)],
        compiler_params=_cparams(("parallel", "arbitrary")),
        name="lnmod_mm",
    )(x2, mod3, mod3, w_bf16, bias.reshape(1, n))


def _prep_kernel(z_ref, zp_ref, zn_ref, mu_ref, w2_ref, a2_ref, g2_ref, w0_ref, a0_ref,
                 r_ref, k_ref, v_ref, a_ref, w_ref, g_ref, *, d_rwkv):
    i = pl.program_id(1)
    nt = pl.num_programs(1)
    z = z_ref[0]
    tt = z.shape[0]
    rows = lax.broadcasted_iota(jnp.int32, (tt, 1), 0)
    halo_p = jnp.where(i == 0, 0.0, zp_ref[0, 7:8, :])
    halo_n = jnp.where(i == nt - 1, 0.0, zn_ref[0, 0:1, :])
    prev = jnp.where(rows == 0, halo_p, pltpu.roll(z, 1, 0))
    nxt = jnp.where(rows == tt - 1, halo_n, pltpu.roll(z, tt - 1, 0))
    zs = z + mu_ref[0:1, :] * (prev - z) + mu_ref[1:2, :] * (nxt - z)
    d = d_rwkv
    r_ref[0] = zs[:, 0:d]
    k_ref[0] = zs[:, d:2 * d]
    v_ref[0] = zs[:, 2 * d:3 * d]
    wd = zs[:, 3 * d:3 * d + LANES]
    ad = zs[:, 3 * d + LANES:3 * d + 2 * LANES]
    gd = zs[:, 3 * d + 2 * LANES:3 * d + 4 * LANES]
    g_ref[0] = jnp.dot(_sigmoid(gd).astype(BF16), g2_ref[...], preferred_element_type=F32)
    lw = jnp.dot(jnp.tanh(wd).astype(BF16), w2_ref[...], preferred_element_type=F32)
    la = jnp.dot(ad.astype(BF16), a2_ref[...], preferred_element_type=F32)
    for dr in range(N_DIR):
        y = -(w0_ref[dr:dr + 1, :] + lw[:, dr * d:(dr + 1) * d])
        softplus = jnp.maximum(y, 0.0) + jnp.log(1.0 + jnp.exp(-jnp.abs(y)))
        w_log = -softplus - 0.5
        w_ref[dr, 0] = jnp.exp(-jnp.exp(w_log))
        a_ref[dr, 0] = _sigmoid(a0_ref[dr:dr + 1, :] + la[:, dr * d:(dr + 1) * d])


def _prep(z3, mu_p, w2blk, a2blk, g2p, w0, a0, tt, d_rwkv):
    b, t, _ = z3.shape
    nt = t // tt
    tb = tt // 8
    nat = jax.ShapeDtypeStruct((b, t, d_rwkv), F32)
    nat2 = jax.ShapeDtypeStruct((N_DIR, b, t, d_rwkv), F32)
    full = lambda a: pl.BlockSpec(a.shape, lambda bi, i: (0,) * a.ndim)
    o1 = pl.BlockSpec((1, tt, d_rwkv), lambda bi, i: (bi, i, 0))
    o2 = pl.BlockSpec((N_DIR, 1, tt, d_rwkv), lambda bi, i: (0, bi, i, 0))
    return pl.pallas_call(
        functools.partial(_prep_kernel, d_rwkv=d_rwkv),
        grid=(b, nt),
        in_specs=[pl.BlockSpec((1, tt, ZR_PAD), lambda bi, i: (bi, i, 0)),
                  pl.BlockSpec((1, 8, ZR_PAD), lambda bi, i: (bi, jnp.maximum(i * tb - 1, 0), 0)),
                  pl.BlockSpec((1, 8, ZR_PAD), lambda bi, i: (bi, jnp.minimum((i + 1) * tb, t // 8 - 1), 0)),
                  full(mu_p), full(w2blk), full(a2blk), full(g2p), full(w0), full(a0)],
        out_specs=[o1, o1, o1, o2, o2, o1],
        out_shape=[nat, nat, nat, nat2, nat2, nat],
        compiler_params=_cparams(("parallel", "arbitrary")),
        name="prep",
    )(z3, z3, z3, mu_p, w2blk, a2blk, g2p, w0, a0)


def _scan_kernel(r_ref, k_ref, v_ref, a_ref, w_ref, kk_ref, ka_ref, y_ref, s_ref, *, tt, n_v):
    d = pl.program_id(0)

    @pl.when(pl.program_id(1) == 0)
    def _():
        s_ref[...] = jnp.zeros_like(s_ref)

    k_k = kk_ref[...]
    k_a = ka_ref[...]

    def step(j, carry):
        t = jnp.where(d == 0, j, tt - 1 - j)
        r = r_ref[t]
        k = k_ref[t]
        a = a_ref[0, t]
        w = w_ref[0, t]
        kk = k * k_k
        nrm = jnp.sqrt(jnp.sum(kk * kk, axis=0, keepdims=True))
        kk = kk / jnp.maximum(nrm, 1e-12)
        kd = k * (1.0 + (a - 1.0) * k_a)
        bvec = kk * a
        for g0 in range(0, n_v, 8):
            ys = []
            for vi in range(g0, g0 + 8):
                sv = s_ref[vi]
                sa = -jnp.sum(sv * kk, axis=0, keepdims=True)
                sn = sv * w + sa * bvec + v_ref[t, pl.ds(vi, 1), :] * kd
                s_ref[vi] = sn
                ys.append(jnp.sum(sn * r, axis=0, keepdims=True))
            y_ref[0, t, pl.ds(g0, 8), :] = jnp.concatenate(ys, axis=0)
        return carry

    lax.fori_loop(0, tt, step, 0)


def _scan(r, k, v, a, w, kk_t, ka_t, tt, nb_ctx):
    t_tot, n, c = r.shape
    nb = t_tot // tt
    nb_lat = nb - nb_ctx

    def blk(d, i):
        bwd = jnp.where(i < nb_ctx, nb_ctx - 1 - i, nb_ctx + nb_lat - 1 - (i - nb_ctx))
        return jnp.where(d == 0, i, bwd)

    s1 = pl.BlockSpec((tt, n, c), lambda d, i: (blk(d, i), 0, 0))
    s2 = pl.BlockSpec((1, tt, n, c), lambda d, i: (d, blk(d, i), 0, 0))
    const = pl.BlockSpec((n, c), lambda d, i: (0, 0))
    return pl.pallas_call(
        functools.partial(_scan_kernel, tt=tt, n_v=n),
        grid=(N_DIR, nb),
        in_specs=[s1, s1, s1, s2, s2, const, const],
        out_specs=s2,
        out_shape=jax.ShapeDtypeStruct((N_DIR, t_tot, n, c), F32),
        scratch_shapes=[pltpu.VMEM((n, n, c), F32)],
        compiler_params=_cparams(("arbitrary", "arbitrary")),
        name="scan",
    )(r, k, v, a, w, kk_t, ka_t)


def _post_kernel(y_ref, r_ref, k_ref, v_ref, a_ref, ka_ref, rk_ref, g_ref, b_ref, o_ref):
    y = y_ref[0] + y_ref[1]
    mu = jnp.mean(y, axis=1, keepdims=True)
    yc = y - mu
    var = jnp.mean(yc * yc, axis=1, keepdims=True)
    yn = yc * lax.rsqrt(var + GN_EPS) * g_ref[...] + b_ref[...]
    r = r_ref[...]
    k = k_ref[...]
    k_a = ka_ref[...]
    rk = 0.0
    for dr in range(N_DIR):
        kd = k * (1.0 + (a_ref[dr] - 1.0) * k_a)
        rk = rk + jnp.sum(r * kd * rk_ref[...], axis=1, keepdims=True)
    o_ref[...] = yn + rk * v_ref[...]


def _post(y, r, k, v, a, ka_t, rk_t, g_t, b_t, tt, t_off):
    _, t_tot, n, c = y.shape
    ob = t_off // tt
    nb = (t_tot - t_off) // tt
    s1 = pl.BlockSpec((tt, n, c), lambda i: (i + ob, 0, 0))
    s2 = pl.BlockSpec((N_DIR, tt, n, c), lambda i: (0, i + ob, 0, 0))
    const = pl.BlockSpec((n, c), lambda i: (0, 0))
    return pl.pallas_call(
        _post_kernel,
        grid=(nb,),
        in_specs=[s2, s1, s1, s1, s2, const, const, const, const],
        out_specs=pl.BlockSpec((tt, n, c), lambda i: (i, 0, 0)),
        out_shape=jax.ShapeDtypeStruct((t_tot - t_off, n, c), F32),
        compiler_params=_cparams(("parallel",)),
        name="post",
    )(y, r, k, v, a, ka_t, rk_t, g_t, b_t)


def _conv_kernel(za_ref, zg_ref, cw_ref, cb_ref, lg_ref, lb_ref, wo_ref, bo_ref, o_ref):
    h = za_ref[...] * _sigmoid(zg_ref[...])
    tm = h.shape[0]
    pos = lax.broadcasted_iota(jnp.int32, (tm, 1), 0) % GRID_W
    half = CONV_WIDTH // 2
    acc = jnp.zeros_like(h)
    for j in range(CONV_WIDTH):
        off = j - half
        sh = pltpu.roll(h, (-off) % tm, 0) if off != 0 else h
        ok = jnp.logical_and(pos + off >= 0, pos + off < GRID_W)
        acc = acc + jnp.where(ok, sh, 0.0) * cw_ref[j:j + 1, :]
    acc = acc + cb_ref[...]
    hn = _ln(acc) * lg_ref[...] + lb_ref[...]
    act = hn * _sigmoid(hn)
    o_ref[...] = jnp.dot(act.astype(BF16), wo_ref[...], preferred_element_type=F32) + bo_ref[...]


def _conv(z, col_a, col_g, cw, cb, lg, lb, wo_bf16, bo, tm):
    n_rows = z.shape[0]
    dc = cw.shape[1]
    dm = wo_bf16.shape[1]
    full = lambda a: pl.BlockSpec(a.shape, lambda m: (0,) * a.ndim)
    cb, lg, lb, bo = cb.reshape(1, dc), lg.reshape(1, dc), lb.reshape(1, dc), bo.reshape(1, dm)
    return pl.pallas_call(
        _conv_kernel,
        grid=(n_rows // tm,),
        in_specs=[pl.BlockSpec((tm, dc), lambda m: (m, col_a)),
                  pl.BlockSpec((tm, dc), lambda m: (m, col_g)),
                  full(cw), full(cb), full(lg), full(lb), full(wo_bf16), full(bo)],
        out_specs=pl.BlockSpec((tm, dm), lambda m: (m, 0)),
        out_shape=jax.ShapeDtypeStruct((n_rows, dm), F32),
        compiler_params=_cparams(("parallel",)),
        name="conv",
    )(z, z, cw, cb, lg, lb, wo_bf16, bo)


def _merge_kernel(o_ref, g_ref, yc_ref, z1_ref, z2_ref, x_ref, gate_ref, wr_ref, wo_ref, bo_ref,
                  lg_ref, lb_ref, out_ref):
    y_rwkv = jnp.dot((o_ref[...] * g_ref[...]).astype(BF16), wr_ref[...], preferred_element_type=F32)
    mix = _sigmoid(z1_ref[...]) * yc_ref[...] + _sigmoid(z2_ref[...]) * y_rwkv
    out = jnp.dot(mix.astype(BF16), wo_ref[...], preferred_element_type=F32) + bo_ref[...]
    h = DEEPNORM_ALPHA * x_ref[...] + gate_ref[0] * out
    out_ref[...] = _ln(h) * lg_ref[...] + lb_ref[...]


def _merge(o_nat, g_nat, y_conv, z, col_g1, x2, mod3, wr_bf16, wo_bf16, bo, lg, lb, tm, rows_per_batch):
    n_rows, dm = x2.shape
    dr = o_nat.shape[1]
    full = lambda a: pl.BlockSpec(a.shape, lambda m: (0,) * a.ndim)
    bo, lg, lb = bo.reshape(1, dm), lg.reshape(1, dm), lb.reshape(1, dm)
    tiles_per_batch = rows_per_batch // tm
    return pl.pallas_call(
        _merge_kernel,
        grid=(n_rows // tm,),
        in_specs=[pl.BlockSpec((tm, dr), lambda m: (m, 0)),
                  pl.BlockSpec((tm, dr), lambda m: (m, 0)),
                  pl.BlockSpec((tm, dm), lambda m: (m, 0)),
                  pl.BlockSpec((tm, dm), lambda m: (m, col_g1)),
                  pl.BlockSpec((tm, dm), lambda m: (m, col_g1 + 1)),
                  pl.BlockSpec((tm, dm), lambda m: (m, 0)),
                  pl.BlockSpec((1, 1, dm), lambda m: ((m // tiles_per_batch) * 6 + 2, 0, 0)),
                  full(wr_bf16), full(wo_bf16), full(bo), full(lg), full(lb)],
        out_specs=pl.BlockSpec((tm, dm), lambda m: (m, 0)),
        out_shape=jax.ShapeDtypeStruct((n_rows, dm), F32),
        compiler_params=_cparams(("parallel",)),
        name="merge",
    )(o_nat, g_nat, y_conv, z, z, x2, mod3, wr_bf16, wo_bf16, bo, lg, lb)


def _router_kernel(x_ref, sh_ref, sc_ref, wr_ref, br_ref, u_ref, idx_ref, gate_ref):
    u = _ln(x_ref[...]) * (1.0 + sc_ref[0]) + sh_ref[0]
    u_ref[...] = u.astype(BF16)
    logits = jnp.dot(u, wr_ref[...], preferred_element_type=F32, precision=HIGHEST) + br_ref[...]
    lane = lax.broadcasted_iota(jnp.int32, logits.shape, 1)
    vals, idxs = [], []
    l = logits
    for _ in range(TOP_K):
        m = jnp.max(l, axis=-1, keepdims=True)
        ix = jnp.min(jnp.where(l == m, lane, LANES), axis=-1, keepdims=True)
        vals.append(m)
        idxs.append(ix)
        l = jnp.where(lane == ix, -jnp.inf, l)
    es = [jnp.exp(vv - vals[0]) for vv in vals]
    den = es[0] + es[1] + es[2] + es[3]
    idx_out = jnp.zeros(logits.shape, jnp.int32)
    gate_out = jnp.zeros(logits.shape, F32)
    for j in range(TOP_K):
        idx_out = jnp.where(lane == j, idxs[j], idx_out)
        gate_out = jnp.where(lane == j, es[j] / den, gate_out)
    idx_ref[...] = idx_out
    gate_ref[...] = gate_out


def _router(x_mid, mod3, w_router_p, b_router_p, tm, rows_per_batch):
    n_rows, dm = x_mid.shape
    tiles_per_batch = rows_per_batch // tm
    full = lambda a: pl.BlockSpec(a.shape, lambda m: (0,) * a.ndim)
    row = pl.BlockSpec((tm, dm), lambda m: (m, 0))
    lane_out = pl.BlockSpec((tm, LANES), lambda m: (m, 0))
    return pl.pallas_call(
        _router_kernel,
        grid=(n_rows // tm,),
        in_specs=[row,
                  pl.BlockSpec((1, 1, dm), lambda m: ((m // tiles_per_batch) * 6 + 3, 0, 0)),
                  pl.BlockSpec((1, 1, dm), lambda m: ((m // tiles_per_batch) * 6 + 4, 0, 0)),
                  full(w_router_p), full(b_router_p)],
        out_specs=[row, lane_out, lane_out],
        out_shape=[jax.ShapeDtypeStruct((n_rows, dm), BF16),
                   jax.ShapeDtypeStruct((n_rows, LANES), jnp.int32),
                   jax.ShapeDtypeStruct((n_rows, LANES), F32)],
        compiler_params=_cparams(("parallel",)),
        name="router",
    )(x_mid, mod3, mod3, w_router_p, b_router_p)


def _group_changed(te_ref, m):
    return jnp.logical_or(m == 0, te_ref[m] != te_ref[jnp.maximum(m - 1, 0)])


def _gmm1_kernel(te_ref, na_ref, x_ref, wg_ref, wu_ref, bg_ref, bu_ref, o_ref, wgb_ref, wub_ref):
    m = pl.program_id(1)

    @pl.when(_group_changed(te_ref, m))
    def _():
        wgb_ref[...] = wg_ref[0].astype(BF16)
        wub_ref[...] = wu_ref[0].astype(BF16)

    @pl.when(m < na_ref[0])
    def _():
        x = x_ref[...]
        gate = jnp.dot(x, wgb_ref[...], preferred_element_type=F32) + bg_ref[0]
        up = jnp.dot(x, wub_ref[...], preferred_element_type=F32) + bu_ref[0]
        gate = jnp.minimum(gate, SWIGLU_LIMIT)
        up = jnp.clip(up, -SWIGLU_LIMIT, SWIGLU_LIMIT)
        hid = (up + 1.0) * (gate * _sigmoid(SWIGLU_ALPHA * gate))
        o_ref[...] = hid.astype(BF16)


def _gmm1(tile_expert, n_active, xs, w_gate_up, b_gate_up3, tm, th):
    p_pad, d = xs.shape
    de = w_gate_up.shape[2] // 2
    nj = de // th
    grid_spec = pltpu.PrefetchScalarGridSpec(
        num_scalar_prefetch=2,
        grid=(nj, p_pad // tm),
        in_specs=[pl.BlockSpec((tm, d), lambda j, m, te, na: (m, 0)),
                  pl.BlockSpec((1, d, th), lambda j, m, te, na: (te[m], 0, j)),
                  pl.BlockSpec((1, d, th), lambda j, m, te, na: (te[m], 0, j + nj)),
                  pl.BlockSpec((1, 1, th), lambda j, m, te, na: (te[m], 0, j)),
                  pl.BlockSpec((1, 1, th), lambda j, m, te, na: (te[m], 0, j + nj))],
        out_specs=pl.BlockSpec((tm, th), lambda j, m, te, na: (m, j)),
        scratch_shapes=[pltpu.VMEM((d, th), BF16), pltpu.VMEM((d, th), BF16)],
    )
    return pl.pallas_call(
        _gmm1_kernel,
        grid_spec=grid_spec,
        out_shape=jax.ShapeDtypeStruct((p_pad, de), BF16),
        compiler_params=_cparams(("arbitrary", "arbitrary")),
        name="gmm1",
    )(tile_expert, n_active, xs, w_gate_up, w_gate_up, b_gate_up3, b_gate_up3)


def _gmm2_kernel(te_ref, na_ref, h_ref, w_ref, b_ref, o_ref, wb_ref):
    m = pl.program_id(1)

    @pl.when(_group_changed(te_ref, m))
    def _():
        wb_ref[...] = w_ref[0].astype(BF16)

    @pl.when(m < na_ref[0])
    def _():
        o_ref[...] = jnp.dot(h_ref[...], wb_ref[...], preferred_element_type=F32) + b_ref[0]


def _gmm2(tile_expert, n_active, hid, w_down, b_down3, tm, tn):
    p_pad, de = hid.shape
    d = w_down.shape[2]
    grid_spec = pltpu.PrefetchScalarGridSpec(
        num_scalar_prefetch=2,
        grid=(d // tn, p_pad // tm),
        in_specs=[pl.BlockSpec((tm, de), lambda j, m, te, na: (m, 0)),
                  pl.BlockSpec((1, de, tn), lambda j, m, te, na: (te[m], 0, j)),
                  pl.BlockSpec((1, 1, tn), lambda j, m, te, na: (te[m], 0, j))],
        out_specs=pl.BlockSpec((tm, tn), lambda j, m, te, na: (m, j)),
        scratch_shapes=[pltpu.VMEM((de, tn), BF16)],
    )
    return pl.pallas_call(
        _gmm2_kernel,
        grid_spec=grid_spec,
        out_shape=jax.ShapeDtypeStruct((p_pad, d), F32),
        compiler_params=_cparams(("arbitrary", "arbitrary")),
        name="gmm2",
    )(tile_expert, n_active, hid, w_down, b_down3)


def _final_kernel(x_ref, y_ref, gate_ref, lg_ref, lb_ref, o_ref):
    h = DEEPNORM_ALPHA * x_ref[...] + gate_ref[0] * y_ref[...]
    o_ref[...] = _ln(h) * lg_ref[...] + lb_ref[...]


def _final(x_mid, moe, mod3, lg, lb, tm, rows_per_batch):
    n_rows, dm = x_mid.shape
    tiles_per_batch = rows_per_batch // tm
    row = pl.BlockSpec((tm, dm), lambda m: (m, 0))
    vec = pl.BlockSpec((1, dm), lambda m: (0, 0))
    return pl.pallas_call(
        _final_kernel,
        grid=(n_rows // tm,),
        in_specs=[row, row,
                  pl.BlockSpec((1, 1, dm), lambda m: ((m // tiles_per_batch) * 6 + 5, 0, 0)),
                  vec, vec],
        out_specs=row,
        out_shape=jax.ShapeDtypeStruct((n_rows, dm), F32),
        compiler_params=_cparams(("parallel",)),
        name="final",
    )(x_mid, moe, mod3, lg.reshape(1, dm), lb.reshape(1, dm))


def _to_scan_layout(a, n_heads):
    *lead, b, t, _ = a.shape
    nl = len(lead)
    a = a.reshape(*lead, b, t, n_heads, HEAD_SIZE)
    perm = tuple(range(nl)) + (nl + 1, nl + 3, nl + 0, nl + 2)
    return a.transpose(perm).reshape(*lead, t, HEAD_SIZE, b * n_heads)


def _head_tile(p, n_heads, batch):
    return jnp.tile(p.reshape(n_heads, HEAD_SIZE).T, (1, batch))


def _pick(n, pref):
    t = min(n, pref)
    while n % t:
        t //= 2
    return t


def kernel(x, c, ctx, c_ctx, w_ada, b_ada, w_in, b_in, shift_mu, conv_w, conv_b, conv_ln_g, conv_ln_b,
           w_conv_o, b_conv_o, w0, w2, a0, a2, g2, k_k, k_a, r_k, lnx_g, lnx_b, w_rwkv_o, w_out, b_out,
           ln1_g, ln1_b, w_router, b_router, w_gate_up, b_gate_up, w_down, b_down, ln2_g, ln2_b):
    bsz, seq, dm = x.shape
    n_ctx = ctx.shape[1]
    dc = conv_w.shape[2]
    dr = k_k.shape[1]
    n_heads = dr // HEAD_SIZE
    d_shift = shift_mu.shape[2]
    n_exp = w_router.shape[2]
    l = 0
    n_tok = bsz * seq

    c_rows = jnp.zeros((16, dm), F32).at[:bsz].set(c).at[bsz].set(c_ctx)
    mod = _ada(c_rows, w_ada[l], b_ada[l])
    mod3 = mod.reshape(16 * 6, 1, dm)

    pad = ZR_PAD - d_shift
    wl = w_in[l]
    w_p = jnp.concatenate([wl[:, 2 * dc:2 * dc + d_shift], jnp.zeros((dm, pad), F32),
                           wl[:, :2 * dc], wl[:, 2 * dc + d_shift:]], axis=1).astype(BF16)
    bl = b_in[l]
    b_p = jnp.concatenate([bl[2 * dc:2 * dc + d_shift], jnp.zeros((pad,), F32),
                           bl[:2 * dc], bl[2 * dc + d_shift:]])
    x2 = x.reshape(n_tok, dm)
    tm_in = _pick(seq, 1024)
    z_l = _lnmod_mm(x2, mod3, lambda m: (m * tm_in) // seq, 0, w_p, b_p, tm_in, 512)
    tm_c = _pick(n_ctx, 1024)
    z_c = _lnmod_mm(ctx.reshape(bsz * n_ctx, dm), mod3, lambda m: bsz, 0,
                    w_p[:, :ZR_PAD], b_p[:ZR_PAD], tm_c, 512)

    mu_p = jnp.pad(shift_mu[l], ((0, 0), (0, pad)))
    w2blk = jnp.zeros((LANES, N_DIR * dr), F32)
    a2blk = jnp.zeros((LANES, N_DIR * dr), F32)
    for d in range(N_DIR):
        w2blk = w2blk.at[d * D_LORA:(d + 1) * D_LORA, d * dr:(d + 1) * dr].set(w2[l, d])
        a2blk = a2blk.at[d * D_LORA:(d + 1) * D_LORA, d * dr:(d + 1) * dr].set(a2[l, d])
    g2p = jnp.zeros((2 * LANES, dr), F32).at[:D_GATE_LORA].set(g2[l])
    prep_args = (mu_p, w2blk.astype(BF16), a2blk.astype(BF16), g2p.astype(BF16), w0[l], a0[l])
    tt_p = _pick(seq, 256)
    r_l, k_l, v_l, a_l, w_l, g_l = _prep(z_l.reshape(bsz, seq, -1), *prep_args, tt_p, dr)
    r_c, k_c, v_c, a_c, w_c, _ = _prep(z_c.reshape(bsz, n_ctx, -1), *prep_args, _pick(n_ctx, 256), dr)

    cat = lambda u, v_: jnp.concatenate([_to_scan_layout(u, n_heads), _to_scan_layout(v_, n_heads)], axis=-3)
    r_s, k_s, v_s, a_s, w_s = cat(r_c, r_l), cat(k_c, k_l), cat(v_c, v_l), cat(a_c, a_l), cat(w_c, w_l)
    ka_t = _head_tile(k_a[l], n_heads, bsz)
    tt_s = _pick(n_ctx, 32)
    y_s = _scan(r_s, k_s, v_s, a_s, w_s, _head_tile(k_k[l], n_heads, bsz), ka_t, tt_s, n_ctx // tt_s)
    o_s = _post(y_s, r_s, k_s, v_s, a_s, ka_t, jnp.tile(r_k[l].T, (1, bsz)),
                _head_tile(lnx_g[l], n_heads, bsz), _head_tile(lnx_b[l], n_heads, bsz), tt_s, n_ctx)
    o_nat = o_s.reshape(seq, HEAD_SIZE, bsz, n_heads).transpose(2, 0, 3, 1).reshape(n_tok, dr)

    zcol = ZR_PAD // dc
    y_conv = _conv(z_l, zcol, zcol + 1, conv_w[l], conv_b[l], conv_ln_g[l], conv_ln_b[l],
                   w_conv_o[l].astype(BF16), b_conv_o[l], _pick(seq, 512))
    gcol = (ZR_PAD + 2 * dc) // dm
    tm_m = _pick(seq, 128)
    x_mid = _merge(o_nat, g_l.reshape(n_tok, dr), y_conv, z_l, gcol, x2, mod3, w_rwkv_o[l].astype(BF16),
                   w_out[l].astype(BF16), b_out[l], ln1_g[l], ln1_b[l], tm_m, seq)

    w_r_p = jnp.zeros((dm, LANES), F32).at[:, :n_exp].set(w_router[l])
    b_r_p = jnp.full((1, LANES), -1e30, F32).at[0, :n_exp].set(b_router[l])
    u_bf, idx_l, gate_l = _router(x_mid, mod3, w_r_p, b_r_p, tm_m, seq)
    top_idx = idx_l[:, :TOP_K]
    gates = gate_l[:, :TOP_K]

    tm_e = EXPERT_ROW_TILE
    n_pair = n_tok * TOP_K
    n_tiles = -(-n_pair // tm_e) + n_exp
    p_pad = n_tiles * tm_e
    flat_e = top_idx.reshape(-1)
    order = jnp.argsort(flat_e, stable=True).astype(jnp.int32)
    sorted_e = flat_e[order]
    counts = jnp.sum(flat_e[:, None] == jnp.arange(n_exp, dtype=jnp.int32)[None, :], axis=0).astype(jnp.int32)
    padded = ((counts + tm_e - 1) // tm_e) * tm_e
    pend = jnp.cumsum(padded)
    pstart = pend - padded
    start = jnp.cumsum(counts) - counts
    dest = pstart[sorted_e] + jnp.arange(n_pair, dtype=jnp.int32) - start[sorted_e]
    row_token = jnp.zeros((p_pad,), jnp.int32).at[dest].set(order // TOP_K)
    pos = jnp.zeros((n_pair,), jnp.int32).at[order].set(dest)
    n_active = (pend[-1] // tm_e).astype(jnp.int32).reshape(1)
    tile_expert = jnp.searchsorted(pend, jnp.arange(n_tiles, dtype=jnp.int32) * tm_e, side="right")
    last_e = jnp.searchsorted(pend, pend[-1] - 1, side="right")
    tile_expert = jnp.minimum(tile_expert, last_e).astype(jnp.int32)

    xs = u_bf[row_token]
    hid = _gmm1(tile_expert, n_active, xs, w_gate_up[l], b_gate_up[l].reshape(n_exp, 1, -1), tm_e, 512)
    ys = _gmm2(tile_expert, n_active, hid, w_down[l], b_down[l].reshape(n_exp, 1, -1), tm_e, 1024)
    moe = jnp.sum(ys[pos].reshape(n_tok, TOP_K, dm) * gates[:, :, None], axis=1)

    out = _final(x_mid, moe, mod3, ln2_g[l], ln2_b[l], _pick(seq, 512), seq)
    return out.reshape(bsz, seq, dm)
```

```python
import functools

import jax
import jax.numpy as jnp
from jax import lax
from jax.experimental import pallas as pl
from jax.experimental.pallas import tpu as pltpu

F32 = jnp.float32
BF16 = jnp.bfloat16
HIGHEST = lax.Precision.HIGHEST

GRID_W = 64
CONV_WIDTH = 31
HEAD_SIZE = 64
D_LORA = 64
D_GATE_LORA = 160
N_DIR = 2
GN_EPS = 64e-5
LN_EPS = 1e-5
TOP_K = 4
SWIGLU_LIMIT = 7.0
SWIGLU_ALPHA = 1.702
DEPTH = 1
DEEPNORM_ALPHA = (2.0 * DEPTH) ** 0.25

LANES = 128
SUBLANES = 8
ZR_PAD = 4096
VMEM_LIMIT = 56 * 1024 * 1024
EXPERT_ROW_TILE = 512


def _cparams(sem):
    return pltpu.CompilerParams(dimension_semantics=sem, vmem_limit_bytes=VMEM_LIMIT)


def _ln(x):
    mu = jnp.mean(x, axis=-1, keepdims=True)
    xc = x - mu
    var = jnp.mean(xc * xc, axis=-1, keepdims=True)
    return xc * lax.rsqrt(var + LN_EPS)


def _sigmoid(x):
    return 1.0 / (1.0 + jnp.exp(-x))


def _ada_kernel(c_ref, w_ref, b_ref, o_ref):
    c = c_ref[...]
    s = c * _sigmoid(c)
    o_ref[...] = jnp.dot(s, w_ref[...], preferred_element_type=F32, precision=HIGHEST) + b_ref[...]


def _ada(c_rows, w, b):
    rows, d = c_rows.shape
    n = w.shape[1]
    tn = 1024
    return pl.pallas_call(
        _ada_kernel,
        grid=(n // tn,),
        in_specs=[pl.BlockSpec((rows, d), lambda j: (0, 0)),
                  pl.BlockSpec((d, tn), lambda j: (0, j)),
                  pl.BlockSpec((1, tn), lambda j: (0, j))],
        out_specs=pl.BlockSpec((rows, tn), lambda j: (0, j)),
        out_shape=jax.ShapeDtypeStruct((rows, n), F32),
        compiler_params=_cparams(("arbitrary",)),
        name="ada",
    )(c_rows, w, b.reshape(1, n))


def _lnmod_mm_kernel(x_ref, sh_ref, sc_ref, w_ref, b_ref, o_ref, xn_ref):
    @pl.when(pl.program_id(1) == 0)
    def _():
        xn = _ln(x_ref[...]) * (1.0 + sc_ref[0]) + sh_ref[0]
        xn_ref[...] = xn.astype(BF16)

    o_ref[...] = jnp.dot(xn_ref[...], w_ref[...], preferred_element_type=F32) + b_ref[...]


def _lnmod_mm(x2, mod3, row_of_tile, shift_slot, w_bf16, bias, tm, tn):
    m_rows, d = x2.shape
    n = w_bf16.shape[1]
    return pl.pallas_call(
        _lnmod_mm_kernel,
        grid=(m_rows // tm, n // tn),
        in_specs=[pl.BlockSpec((tm, d), lambda m, j: (m, 0)),
                  pl.BlockSpec((1, 1, d), lambda m, j: (row_of_tile(m) * 6 + shift_slot, 0, 0)),
                  pl.BlockSpec((1, 1, d), lambda m, j: (row_of_tile(m) * 6 + shift_slot + 1, 0, 0)),
                  pl.BlockSpec((d, tn), lambda m, j: (0, j)),
                  pl.BlockSpec((1, tn), lambda m, j: (0, j))],
        out_specs=pl.BlockSpec((tm, tn), lambda m, j: (m, j)),
        out_shape=jax.ShapeDtypeStruct((m_rows, n), F32),
        scratch_shapes=[pltpu.VMEM((tm, d), BF16)],
        compiler_params=_cparams(("parallel", "arbitrary")),
        name="lnmod_mm",
    )(x2, mod3, mod3, w_bf16, bias.reshape(1, n))


def _prep_kernel(z_ref, zp_ref, zn_ref, mu_ref, w2_ref, a2_ref, g2_ref, w0_ref, a0_ref,
                 r_ref, k_ref, v_ref, a_ref, w_ref, g_ref, *, d_rwkv):
    i = pl.program_id(1)
    nt = pl.num_programs(1)
    z = z_ref[0]
    tt = z.shape[0]
    rows = lax.broadcasted_iota(jnp.int32, (tt, 1), 0)
    halo_p = jnp.where(i == 0, 0.0, zp_ref[0, 7:8, :])
    halo_n = jnp.where(i == nt - 1, 0.0, zn_ref[0, 0:1, :])
    prev = jnp.where(rows == 0, halo_p, pltpu.roll(z, 1, 0))
    nxt = jnp.where(rows == tt - 1, halo_n, pltpu.roll(z, tt - 1, 0))
    zs = z + mu_ref[0:1, :] * (prev - z) + mu_ref[1:2, :] * (nxt - z)
    d = d_rwkv
    r_ref[0] = zs[:, 0:d]
    k_ref[0] = zs[:, d:2 * d]
    v_ref[0] = zs[:, 2 * d:3 * d]
    wd = zs[:, 3 * d:3 * d + LANES]
    ad = zs[:, 3 * d + LANES:3 * d + 2 * LANES]
    gd = zs[:, 3 * d + 2 * LANES:3 * d + 4 * LANES]
    g_ref[0] = jnp.dot(_sigmoid(gd).astype(BF16), g2_ref[...], preferred_element_type=F32)
    lw = jnp.dot(jnp.tanh(wd).astype(BF16), w2_ref[...], preferred_element_type=F32)
    la = jnp.dot(ad.astype(BF16), a2_ref[...], preferred_element_type=F32)
    for dr in range(N_DIR):
        y = -(w0_ref[dr:dr + 1, :] + lw[:, dr * d:(dr + 1) * d])
        softplus = jnp.maximum(y, 0.0) + jnp.log(1.0 + jnp.exp(-jnp.abs(y)))
        w_log = -softplus - 0.5
        w_ref[dr, 0] = jnp.exp(-jnp.exp(w_log))
        a_ref[dr, 0] = _sigmoid(a0_ref[dr:dr + 1, :] + la[:, dr * d:(dr + 1) * d])


def _prep(z3, mu_p, w2blk, a2blk, g2p, w0, a0, tt, d_rwkv):
    b, t, _ = z3.shape
    nt = t // tt
    tb = tt // 8
    nat = jax.ShapeDtypeStruct((b, t, d_rwkv), F32)
    nat2 = jax.ShapeDtypeStruct((N_DIR, b, t, d_rwkv), F32)
    full = lambda a: pl.BlockSpec(a.shape, lambda bi, i: (0,) * a.ndim)
    o1 = pl.BlockSpec((1, tt, d_rwkv), lambda bi, i: (bi, i, 0))
    o2 = pl.BlockSpec((N_DIR, 1, tt, d_rwkv), lambda bi, i: (0, bi, i, 0))
    return pl.pallas_call(
        functools.partial(_prep_kernel, d_rwkv=d_rwkv),
        grid=(b, nt),
        in_specs=[pl.BlockSpec((1, tt, ZR_PAD), lambda bi, i: (bi, i, 0)),
                  pl.BlockSpec((1, 8, ZR_PAD), lambda bi, i: (bi, jnp.maximum(i * tb - 1, 0), 0)),
                  pl.BlockSpec((1, 8, ZR_PAD), lambda bi, i: (bi, jnp.minimum((i + 1) * tb, t // 8 - 1), 0)),
                  full(mu_p), full(w2blk), full(a2blk), full(g2p), full(w0), full(a0)],
        out_specs=[o1, o1, o1, o2, o2, o1],
        out_shape=[nat, nat, nat, nat2, nat2, nat],
        compiler_params=_cparams(("parallel", "arbitrary")),
        name="prep",
    )(z3, z3, z3, mu_p, w2blk, a2blk, g2p, w0, a0)


def _scan_kernel(r_ref, k_ref, v_ref, a_ref, w_ref, kk_ref, ka_ref, y_ref, s_ref, ops_ref, *, tt, n):
    d = pl.program_id(0)
    nvb = n // SUBLANES

    @pl.when(pl.program_id(1) == 0)
    def _():
        s_ref[...] = jnp.zeros_like(s_ref)

    k_k = kk_ref[...]
    k_a = ka_ref[...]

    def step(j, carry):
        t = jnp.where(d == 0, j, tt - 1 - j)
        k = k_ref[t]
        a = a_ref[0, t]
        kk = k * k_k
        nrm = jnp.sqrt(jnp.sum(kk * kk, axis=0, keepdims=True))
        kk = kk / jnp.maximum(nrm, 1e-12)
        ops_ref[0] = kk
        ops_ref[1] = w_ref[0, t]
        ops_ref[2] = -(kk * a)
        ops_ref[3] = k * (1.0 + (a - 1.0) * k_a)
        ops_ref[4] = r_ref[t]
        row = lambda o, kx: ops_ref[o, pl.ds(kx, 1), :]
        tile = lambda kx, vb: s_ref[kx, pl.ds(vb * SUBLANES, SUBLANES), :]
        acc = [[None, None] for _ in range(nvb)]
        for kx in range(n):
            kb = row(0, kx)
            for vb in range(nvb):
                p = tile(kx, vb) * kb
                cur = acc[vb][kx % 2]
                acc[vb][kx % 2] = p if cur is None else cur + p
        skk = [acc[vb][0] + acc[vb][1] for vb in range(nvb)]
        vv = [v_ref[t, pl.ds(vb * SUBLANES, SUBLANES), :] for vb in range(nvb)]
        yacc = [None] * nvb
        for kx in range(n):
            wb, nb_, kdb, rb = row(1, kx), row(2, kx), row(3, kx), row(4, kx)
            for vb in range(nvb):
                sn = tile(kx, vb) * wb + skk[vb] * nb_ + vv[vb] * kdb
                s_ref[kx, pl.ds(vb * SUBLANES, SUBLANES), :] = sn
                q = sn * rb
                yacc[vb] = q if yacc[vb] is None else yacc[vb] + q
        for vb in range(nvb):
            y_ref[0, t, pl.ds(vb * SUBLANES, SUBLANES), :] = yacc[vb]
        return carry

    lax.fori_loop(0, tt, step, 0)


def _scan(r, k, v, a, w, kk_t, ka_t, tt, nb_ctx):
    t_tot, n, c = r.shape
    nb = t_tot // tt
    nb_lat = nb - nb_ctx

    def blk(d, i):
        bwd = jnp.where(i < nb_ctx, nb_ctx - 1 - i, nb_ctx + nb_lat - 1 - (i - nb_ctx))
        return jnp.where(d == 0, i, bwd)

    s1 = pl.BlockSpec((tt, n, c), lambda d, i: (blk(d, i), 0, 0))
    s2 = pl.BlockSpec((1, tt, n, c), lambda d, i: (d, blk(d, i), 0, 0))
    const = pl.BlockSpec((n, c), lambda d, i: (0, 0))
    return pl.pallas_call(
        functools.partial(_scan_kernel, tt=tt, n=n),
        grid=(N_DIR, nb),
        in_specs=[s1, s1, s1, s2, s2, const, const],
        out_specs=s2,
        out_shape=jax.ShapeDtypeStruct((N_DIR, t_tot, n, c), F32),
        scratch_shapes=[pltpu.VMEM((n, n, c), F32), pltpu.VMEM((5, n, c), F32)],
        compiler_params=_cparams(("arbitrary", "arbitrary")),
        name="scan",
    )(r, k, v, a, w, kk_t, ka_t)


def _post_kernel(y_ref, r_ref, k_ref, v_ref, a_ref, ka_ref, rk_ref, g_ref, b_ref, o_ref):
    y = y_ref[0] + y_ref[1]
    mu = jnp.mean(y, axis=1, keepdims=True)
    yc = y - mu
    var = jnp.mean(yc * yc, axis=1, keepdims=True)
    yn = yc * lax.rsqrt(var + GN_EPS) * g_ref[...] + b_ref[...]
    r = r_ref[...]
    k = k_ref[...]
    k_a = ka_ref[...]
    rk = 0.0
    for dr in range(N_DIR):
        kd = k * (1.0 + (a_ref[dr] - 1.0) * k_a)
        rk = rk + jnp.sum(r * kd * rk_ref[...], axis=1, keepdims=True)
    o_ref[...] = yn + rk * v_ref[...]


def _post(y, r, k, v, a, ka_t, rk_t, g_t, b_t, tt, t_off):
    _, t_tot, n, c = y.shape
    ob = t_off // tt
    nb = (t_tot - t_off) // tt
    s1 = pl.BlockSpec((tt, n, c), lambda i: (i + ob, 0, 0))
    s2 = pl.BlockSpec((N_DIR, tt, n, c), lambda i: (0, i + ob, 0, 0))
    const = pl.BlockSpec((n, c), lambda i: (0, 0))
    return pl.pallas_call(
        _post_kernel,
        grid=(nb,),
        in_specs=[s2, s1, s1, s1, s2, const, const, const, const],
        out_specs=pl.BlockSpec((tt, n, c), lambda i: (i, 0, 0)),
        out_shape=jax.ShapeDtypeStruct((t_tot - t_off, n, c), F32),
        compiler_params=_cparams(("parallel",)),
        name="post",
    )(y, r, k, v, a, ka_t, rk_t, g_t, b_t)


def _conv_kernel(za_ref, zg_ref, cw_ref, cb_ref, lg_ref, lb_ref, wo_ref, bo_ref, o_ref):
    h = za_ref[...] * _sigmoid(zg_ref[...])
    tm = h.shape[0]
    pos = lax.broadcasted_iota(jnp.int32, (tm, 1), 0) % GRID_W
    half = CONV_WIDTH // 2
    acc = jnp.zeros_like(h)
    for j in range(CONV_WIDTH):
        off = j - half
        sh = pltpu.roll(h, (-off) % tm, 0) if off != 0 else h
        ok = jnp.logical_and(pos + off >= 0, pos + off < GRID_W)
        acc = acc + jnp.where(ok, sh, 0.0) * cw_ref[j:j + 1, :]
    acc = acc + cb_ref[...]
    hn = _ln(acc) * lg_ref[...] + lb_ref[...]
    act = hn * _sigmoid(hn)
    o_ref[...] = jnp.dot(act.astype(BF16), wo_ref[...], preferred_element_type=F32) + bo_ref[...]


def _conv(z, col_a, col_g, cw, cb, lg, lb, wo_bf16, bo, tm):
    n_rows = z.shape[0]
    dc = cw.shape[1]
    dm = wo_bf16.shape[1]
    full = lambda a: pl.BlockSpec(a.shape, lambda m: (0,) * a.ndim)
    cb, lg, lb, bo = cb.reshape(1, dc), lg.reshape(1, dc), lb.reshape(1, dc), bo.reshape(1, dm)
    return pl.pallas_call(
        _conv_kernel,
        grid=(n_rows // tm,),
        in_specs=[pl.BlockSpec((tm, dc), lambda m: (m, col_a)),
                  pl.BlockSpec((tm, dc), lambda m: (m, col_g)),
                  full(cw), full(cb), full(lg), full(lb), full(wo_bf16), full(bo)],
        out_specs=pl.BlockSpec((tm, dm), lambda m: (m, 0)),
        out_shape=jax.ShapeDtypeStruct((n_rows, dm), F32),
        compiler_params=_cparams(("parallel",)),
        name="conv",
    )(z, z, cw, cb, lg, lb, wo_bf16, bo)


def _merge_kernel(o_ref, g_ref, yc_ref, z1_ref, z2_ref, x_ref, gate_ref, wr_ref, wo_ref, bo_ref,
                  lg_ref, lb_ref, out_ref):
    y_rwkv = jnp.dot((o_ref[...] * g_ref[...]).astype(BF16), wr_ref[...], preferred_element_type=F32)
    mix = _sigmoid(z1_ref[...]) * yc_ref[...] + _sigmoid(z2_ref[...]) * y_rwkv
    out = jnp.dot(mix.astype(BF16), wo_ref[...], preferred_element_type=F32) + bo_ref[...]
    h = DEEPNORM_ALPHA * x_ref[...] + gate_ref[0] * out
    out_ref[...] = _ln(h) * lg_ref[...] + lb_ref[...]


def _merge(o_nat, g_nat, y_conv, z, col_g1, x2, mod3, wr_bf16, wo_bf16, bo, lg, lb, tm, rows_per_batch):
    n_rows, dm = x2.shape
    dr = o_nat.shape[1]
    full = lambda a: pl.BlockSpec(a.shape, lambda m: (0,) * a.ndim)
    bo, lg, lb = bo.reshape(1, dm), lg.reshape(1, dm), lb.reshape(1, dm)
    tiles_per_batch = rows_per_batch // tm
    return pl.pallas_call(
        _merge_kernel,
        grid=(n_rows // tm,),
        in_specs=[pl.BlockSpec((tm, dr), lambda m: (m, 0)),
                  pl.BlockSpec((tm, dr), lambda m: (m, 0)),
                  pl.BlockSpec((tm, dm), lambda m: (m, 0)),
                  pl.BlockSpec((tm, dm), lambda m: (m, col_g1)),
                  pl.BlockSpec((tm, dm), lambda m: (m, col_g1 + 1)),
                  pl.BlockSpec((tm, dm), lambda m: (m, 0)),
                  pl.BlockSpec((1, 1, dm), lambda m: ((m // tiles_per_batch) * 6 + 2, 0, 0)),
                  full(wr_bf16), full(wo_bf16), full(bo), full(lg), full(lb)],
        out_specs=pl.BlockSpec((tm, dm), lambda m: (m, 0)),
        out_shape=jax.ShapeDtypeStruct((n_rows, dm), F32),
        compiler_params=_cparams(("parallel",)),
        name="merge",
    )(o_nat, g_nat, y_conv, z, z, x2, mod3, wr_bf16, wo_bf16, bo, lg, lb)


def _router_kernel(x_ref, sh_ref, sc_ref, wr_ref, br_ref, u_ref, idx_ref, gate_ref):
    u = _ln(x_ref[...]) * (1.0 + sc_ref[0]) + sh_ref[0]
    u_ref[...] = u
    logits = jnp.dot(u, wr_ref[...], preferred_element_type=F32, precision=HIGHEST) + br_ref[...]
    lane = lax.broadcasted_iota(jnp.int32, logits.shape, 1)
    vals, idxs = [], []
    l = logits
    for _ in range(TOP_K):
        m = jnp.max(l, axis=-1, keepdims=True)
        ix = jnp.min(jnp.where(l == m, lane, LANES), axis=-1, keepdims=True)
        vals.append(m)
        idxs.append(ix)
        l = jnp.where(lane == ix, -jnp.inf, l)
    es = [jnp.exp(vv - vals[0]) for vv in vals]
    den = es[0] + es[1] + es[2] + es[3]
    idx_out = jnp.zeros(logits.shape, jnp.int32)
    gate_out = jnp.zeros(logits.shape, F32)
    for j in range(TOP_K):
        idx_out = jnp.where(lane == j, idxs[j], idx_out)
        gate_out = jnp.where(lane == j, es[j] / den, gate_out)
    idx_ref[...] = idx_out
    gate_ref[...] = gate_out


def _router(x_mid, mod3, w_router_p, b_router_p, tm, rows_per_batch):
    n_rows, dm = x_mid.shape
    tiles_per_batch = rows_per_batch // tm
    full = lambda a: pl.BlockSpec(a.shape, lambda m: (0,) * a.ndim)
    row = pl.BlockSpec((tm, dm), lambda m: (m, 0))
    lane_out = pl.BlockSpec((tm, LANES), lambda m: (m, 0))
    return pl.pallas_call(
        _router_kernel,
        grid=(n_rows // tm,),
        in_specs=[row,
                  pl.BlockSpec((1, 1, dm), lambda m: ((m // tiles_per_batch) * 6 + 3, 0, 0)),
                  pl.BlockSpec((1, 1, dm), lambda m: ((m // tiles_per_batch) * 6 + 4, 0, 0)),
                  full(w_router_p), full(b_router_p)],
        out_specs=[row, lane_out, lane_out],
        out_shape=[jax.ShapeDtypeStruct((n_rows, dm), F32),
                   jax.ShapeDtypeStruct((n_rows, LANES), jnp.int32),
                   jax.ShapeDtypeStruct((n_rows, LANES), F32)],
        compiler_params=_cparams(("parallel",)),
        name="router",
    )(x_mid, mod3, mod3, w_router_p, b_router_p)


def _group_changed(te_ref, m):
    return jnp.logical_or(m == 0, te_ref[m] != te_ref[jnp.maximum(m - 1, 0)])


def _gmm1_kernel(te_ref, na_ref, x_ref, wg_ref, wu_ref, bg_ref, bu_ref, o_ref, wgb_ref, wub_ref):
    m = pl.program_id(1)

    @pl.when(_group_changed(te_ref, m))
    def _():
        wgb_ref[...] = wg_ref[0].astype(BF16)
        wub_ref[...] = wu_ref[0].astype(BF16)

    @pl.when(m < na_ref[0])
    def _():
        x = x_ref[...]
        gate = jnp.dot(x, wgb_ref[...], preferred_element_type=F32) + bg_ref[0]
        up = jnp.dot(x, wub_ref[...], preferred_element_type=F32) + bu_ref[0]
        gate = jnp.minimum(gate, SWIGLU_LIMIT)
        up = jnp.clip(up, -SWIGLU_LIMIT, SWIGLU_LIMIT)
        hid = (up + 1.0) * (gate * _sigmoid(SWIGLU_ALPHA * gate))
        o_ref[...] = hid.astype(BF16)

    @pl.when(m >= na_ref[0])
    def _():
        o_ref[...] = jnp.zeros_like(o_ref)


def _gmm1(tile_expert, n_active, xs, w_gate_up, b_gate_up3, tm, th):
    p_pad, d = xs.shape
    de = w_gate_up.shape[2] // 2
    nj = de // th
    grid_spec = pltpu.PrefetchScalarGridSpec(
        num_scalar_prefetch=2,
        grid=(nj, p_pad // tm),
        in_specs=[pl.BlockSpec((tm, d), lambda j, m, te, na: (m, 0)),
                  pl.BlockSpec((1, d, th), lambda j, m, te, na: (te[m], 0, j)),
                  pl.BlockSpec((1, d, th), lambda j, m, te, na: (te[m], 0, j + nj)),
                  pl.BlockSpec((1, 1, th), lambda j, m, te, na: (te[m], 0, j)),
                  pl.BlockSpec((1, 1, th), lambda j, m, te, na: (te[m], 0, j + nj))],
        out_specs=pl.BlockSpec((tm, th), lambda j, m, te, na: (m, j)),
        scratch_shapes=[pltpu.VMEM((d, th), BF16), pltpu.VMEM((d, th), BF16)],
    )
    return pl.pallas_call(
        _gmm1_kernel,
        grid_spec=grid_spec,
        out_shape=jax.ShapeDtypeStruct((p_pad, de), BF16),
        compiler_params=_cparams(("arbitrary", "arbitrary")),
        name="gmm1",
    )(tile_expert, n_active, xs, w_gate_up, w_gate_up, b_gate_up3, b_gate_up3)


def _gmm2_kernel(te_ref, na_ref, h_ref, w_ref, b_ref, o_ref, wb_ref):
    m = pl.program_id(1)

    @pl.when(_group_changed(te_ref, m))
    def _():
        wb_ref[...] = w_ref[0].astype(BF16)

    @pl.when(m < na_ref[0])
    def _():
        o_ref[...] = jnp.dot(h_ref[...], wb_ref[...], preferred_element_type=F32) + b_ref[0]

    @pl.when(m >= na_ref[0])
    def _():
        o_ref[...] = jnp.zeros_like(o_ref)


def _gmm2(tile_expert, n_active, hid, w_down, b_down3, tm, tn):
    p_pad, de = hid.shape
    d = w_down.shape[2]
    grid_spec = pltpu.PrefetchScalarGridSpec(
        num_scalar_prefetch=2,
        grid=(d // tn, p_pad // tm),
        in_specs=[pl.BlockSpec((tm, de), lambda j, m, te, na: (m, 0)),
                  pl.BlockSpec((1, de, tn), lambda j, m, te, na: (te[m], 0, j)),
                  pl.BlockSpec((1, 1, tn), lambda j, m, te, na: (te[m], 0, j))],
        out_specs=pl.BlockSpec((tm, tn), lambda j, m, te, na: (m, j)),
        scratch_shapes=[pltpu.VMEM((de, tn), BF16)],
    )
    return pl.pallas_call(
        _gmm2_kernel,
        grid_spec=grid_spec,
        out_shape=jax.ShapeDtypeStruct((p_pad, d), F32),
        compiler_params=_cparams(("arbitrary", "arbitrary")),
        name="gmm2",
    )(tile_expert, n_active, hid, w_down, b_down3)


def _row_copy(src_hbm, src_row, buf_ref, slot, dst_row, sem_ref):
    return pltpu.make_async_copy(src_hbm.at[pl.ds(src_row, 1), :],
                                 buf_ref.at[slot, pl.ds(dst_row, 1), :], sem_ref.at[slot])


def _wait_rows(src_hbm, buf_ref, slot, sem_ref):
    n = buf_ref.shape[1]
    pltpu.make_async_copy(src_hbm.at[pl.ds(0, n), :], buf_ref.at[slot], sem_ref.at[slot]).wait()


def _dispatch_kernel(rt_ref, na_ref, u_hbm, o_ref, buf_ref, sem_ref, *, tm):
    i = pl.program_id(0)
    na = na_ref[0]

    def issue(tile, slot):
        def body(r, carry):
            _row_copy(u_hbm, rt_ref[tile * tm + r], buf_ref, slot, r, sem_ref).start()
            return carry
        lax.fori_loop(0, tm, body, 0)

    @pl.when(jnp.logical_and(i == 0, na > 0))
    def _():
        issue(0, 0)

    @pl.when(i + 1 < na)
    def _():
        issue(i + 1, (i + 1) % 2)

    @pl.when(i < na)
    def _():
        slot = i % 2
        _wait_rows(u_hbm, buf_ref, slot, sem_ref)
        o_ref[...] = buf_ref[slot].astype(BF16)

    @pl.when(i >= na)
    def _():
        o_ref[...] = jnp.zeros_like(o_ref)


def _dispatch(row_token, n_active, u_f, tm):
    p_pad = row_token.shape[0]
    dm = u_f.shape[1]
    grid_spec = pltpu.PrefetchScalarGridSpec(
        num_scalar_prefetch=2,
        grid=(p_pad // tm,),
        in_specs=[pl.BlockSpec(memory_space=pl.ANY)],
        out_specs=pl.BlockSpec((tm, dm), lambda i, rt, na: (i, 0)),
        scratch_shapes=[pltpu.VMEM((2, tm, dm), F32), pltpu.SemaphoreType.DMA((2,))],
    )
    return pl.pallas_call(
        functools.partial(_dispatch_kernel, tm=tm),
        grid_spec=grid_spec,
        out_shape=jax.ShapeDtypeStruct((p_pad, dm), BF16),
        compiler_params=_cparams(("arbitrary",)),
        name="dispatch",
    )(row_token, n_active, u_f)


def _combine_kernel(pos_ref, ys_hbm, x_ref, gl_ref, gate_ref, lg_ref, lb_ref, o_ref, buf_ref, sem_ref, *, tm):
    i = pl.program_id(0)
    n = pl.num_programs(0)

    def issue(tile, slot):
        def body(r, carry):
            for j in range(TOP_K):
                p = pos_ref[(tile * tm + r) * TOP_K + j]
                _row_copy(ys_hbm, p, buf_ref, slot, j * tm + r, sem_ref).start()
            return carry
        lax.fori_loop(0, tm, body, 0)

    @pl.when(i == 0)
    def _():
        issue(0, 0)

    @pl.when(i + 1 < n)
    def _():
        issue(i + 1, (i + 1) % 2)

    slot = i % 2
    _wait_rows(ys_hbm, buf_ref, slot, sem_ref)
    gl = gl_ref[...]
    moe = gl[:, 0:1] * buf_ref[slot, pl.ds(0, tm), :]
    for j in range(1, TOP_K):
        moe = moe + gl[:, j:j + 1] * buf_ref[slot, pl.ds(j * tm, tm), :]
    h = DEEPNORM_ALPHA * x_ref[...] + gate_ref[0] * moe
    o_ref[...] = _ln(h) * lg_ref[...] + lb_ref[...]


def _combine(pos, ys, x_mid, gate_l, mod3, lg, lb, tm, rows_per_batch):
    n_rows, dm = x_mid.shape
    tiles_per_batch = rows_per_batch // tm
    row = pl.BlockSpec((tm, dm), lambda m, p: (m, 0))
    vec = pl.BlockSpec((1, dm), lambda m, p: (0, 0))
    grid_spec = pltpu.PrefetchScalarGridSpec(
        num_scalar_prefetch=1,
        grid=(n_rows // tm,),
        in_specs=[pl.BlockSpec(memory_space=pl.ANY), row,
                  pl.BlockSpec((tm, LANES), lambda m, p: (m, 0)),
                  pl.BlockSpec((1, 1, dm), lambda m, p: ((m // tiles_per_batch) * 6 + 5, 0, 0)),
                  vec, vec],
        out_specs=row,
        scratch_shapes=[pltpu.VMEM((2, TOP_K * tm, dm), F32), pltpu.SemaphoreType.DMA((2,))],
    )
    return pl.pallas_call(
        functools.partial(_combine_kernel, tm=tm),
        grid_spec=grid_spec,
        out_shape=jax.ShapeDtypeStruct((n_rows, dm), F32),
        compiler_params=_cparams(("arbitrary",)),
        name="combine",
    )(pos, ys, x_mid, gate_l, mod3, lg.reshape(1, dm), lb.reshape(1, dm))


def _to_scan_layout(a, n_heads):
    *lead, b, t, _ = a.shape
    nl = len(lead)
    a = a.reshape(*lead, b, t, n_heads, HEAD_SIZE)
    perm = tuple(range(nl)) + (nl + 1, nl + 3, nl + 0, nl + 2)
    return a.transpose(perm).reshape(*lead, t, HEAD_SIZE, b * n_heads)


def _head_tile(p, n_heads, batch):
    return jnp.tile(p.reshape(n_heads, HEAD_SIZE).T, (1, batch))


def _pick(n, pref):
    t = min(n, pref)
    while n % t:
        t //= 2
    return t


def kernel(x, c, ctx, c_ctx, w_ada, b_ada, w_in, b_in, shift_mu, conv_w, conv_b, conv_ln_g, conv_ln_b,
           w_conv_o, b_conv_o, w0, w2, a0, a2, g2, k_k, k_a, r_k, lnx_g, lnx_b, w_rwkv_o, w_out, b_out,
           ln1_g, ln1_b, w_router, b_router, w_gate_up, b_gate_up, w_down, b_down, ln2_g, ln2_b):
    bsz, seq, dm = x.shape
    n_ctx = ctx.shape[1]
    dc = conv_w.shape[2]
    dr = k_k.shape[1]
    n_heads = dr // HEAD_SIZE
    d_shift = shift_mu.shape[2]
    n_exp = w_router.shape[2]
    l = 0
    n_tok = bsz * seq

    c_rows = jnp.zeros((16, dm), F32).at[:bsz].set(c).at[bsz].set(c_ctx)
    mod = _ada(c_rows, w_ada[l], b_ada[l])
    mod3 = mod.reshape(16 * 6, 1, dm)

    pad = ZR_PAD - d_shift
    wl = w_in[l]
    w_p = jnp.concatenate([wl[:, 2 * dc:2 * dc + d_shift], jnp.zeros((dm, pad), F32),
                           wl[:, :2 * dc], wl[:, 2 * dc + d_shift:]], axis=1).astype(BF16)
    bl = b_in[l]
    b_p = jnp.concatenate([bl[2 * dc:2 * dc + d_shift], jnp.zeros((pad,), F32),
                           bl[:2 * dc], bl[2 * dc + d_shift:]])
    x2 = x.reshape(n_tok, dm)
    tm_in = _pick(seq, 1024)
    z_l = _lnmod_mm(x2, mod3, lambda m: (m * tm_in) // seq, 0, w_p, b_p, tm_in, 512)
    tm_c = _pick(n_ctx, 1024)
    z_c = _lnmod_mm(ctx.reshape(bsz * n_ctx, dm), mod3, lambda m: bsz, 0,
                    w_p[:, :ZR_PAD], b_p[:ZR_PAD], tm_c, 512)

    mu_p = jnp.pad(shift_mu[l], ((0, 0), (0, pad)))
    w2blk = jnp.zeros((LANES, N_DIR * dr), F32)
    a2blk = jnp.zeros((LANES, N_DIR * dr), F32)
    for d in range(N_DIR):
        w2blk = w2blk.at[d * D_LORA:(d + 1) * D_LORA, d * dr:(d + 1) * dr].set(w2[l, d])
        a2blk = a2blk.at[d * D_LORA:(d + 1) * D_LORA, d * dr:(d + 1) * dr].set(a2[l, d])
    g2p = jnp.zeros((2 * LANES, dr), F32).at[:D_GATE_LORA].set(g2[l])
    prep_args = (mu_p, w2blk.astype(BF16), a2blk.astype(BF16), g2p.astype(BF16), w0[l], a0[l])
    tt_p = _pick(seq, 256)
    r_l, k_l, v_l, a_l, w_l, g_l = _prep(z_l.reshape(bsz, seq, -1), *prep_args, tt_p, dr)
    r_c, k_c, v_c, a_c, w_c, _ = _prep(z_c.reshape(bsz, n_ctx, -1), *prep_args, _pick(n_ctx, 256), dr)

    cat = lambda u, v_: jnp.concatenate([_to_scan_layout(u, n_heads), _to_scan_layout(v_, n_heads)], axis=-3)
    r_s, k_s, v_s, a_s, w_s = cat(r_c, r_l), cat(k_c, k_l), cat(v_c, v_l), cat(a_c, a_l), cat(w_c, w_l)
    ka_t = _head_tile(k_a[l], n_heads, bsz)
    tt_s = _pick(n_ctx, 32)
    y_s = _scan(r_s, k_s, v_s, a_s, w_s, _head_tile(k_k[l], n_heads, bsz), ka_t, tt_s, n_ctx // tt_s)
    o_s = _post(y_s, r_s, k_s, v_s, a_s, ka_t, jnp.tile(r_k[l].T, (1, bsz)),
                _head_tile(lnx_g[l], n_heads, bsz), _head_tile(lnx_b[l], n_heads, bsz), tt_s, n_ctx)
    o_nat = o_s.reshape(seq, HEAD_SIZE, bsz, n_heads).transpose(2, 0, 3, 1).reshape(n_tok, dr)

    zcol = ZR_PAD // dc
    y_conv = _conv(z_l, zcol, zcol + 1, conv_w[l], conv_b[l], conv_ln_g[l], conv_ln_b[l],
                   w_conv_o[l].astype(BF16), b_conv_o[l], _pick(seq, 512))
    gcol = (ZR_PAD + 2 * dc) // dm
    tm_m = _pick(seq, 128)
    x_mid = _merge(o_nat, g_l.reshape(n_tok, dr), y_conv, z_l, gcol, x2, mod3, w_rwkv_o[l].astype(BF16),
                   w_out[l].astype(BF16), b_out[l], ln1_g[l], ln1_b[l], tm_m, seq)

    w_r_p = jnp.zeros((dm, LANES), F32).at[:, :n_exp].set(w_router[l])
    b_r_p = jnp.full((1, LANES), -1e30, F32).at[0, :n_exp].set(b_router[l])
    u_f, idx_l, gate_l = _router(x_mid, mod3, w_r_p, b_r_p, tm_m, seq)
    top_idx = idx_l[:, :TOP_K]

    tm_e = EXPERT_ROW_TILE
    n_pair = n_tok * TOP_K
    n_tiles = -(-n_pair // tm_e) + n_exp
    p_pad = n_tiles * tm_e
    eq = top_idx[:, :, None] == jnp.arange(n_exp, dtype=jnp.int32)[None, None, :]
    onehot = jnp.sum(eq, axis=1, dtype=jnp.int32)
    cnt_incl = jnp.cumsum(onehot, axis=0)
    counts = cnt_incl[-1]
    cnt_excl = cnt_incl - onehot
    padded = ((counts + tm_e - 1) // tm_e) * tm_e
    pend = jnp.cumsum(padded)
    pstart = pend - padded
    pos = jnp.sum(jnp.where(eq, (cnt_excl + pstart[None, :])[:, None, :], 0), axis=2).reshape(-1)
    row_token = jnp.zeros((p_pad,), jnp.int32).at[pos].set(jnp.arange(n_pair, dtype=jnp.int32) // TOP_K)
    n_active = (pend[-1] // tm_e).astype(jnp.int32).reshape(1)
    tile_start = jnp.arange(n_tiles, dtype=jnp.int32) * tm_e
    tile_expert = jnp.sum(pend[None, :] <= tile_start[:, None], axis=1, dtype=jnp.int32)
    last_e = jnp.sum(pend <= pend[-1] - 1, dtype=jnp.int32)
    tile_expert = jnp.minimum(tile_expert, last_e)

    xs = _dispatch(row_token, n_active, u_f, tm_e)
    hid = _gmm1(tile_expert, n_active, xs, w_gate_up[l], b_gate_up[l].reshape(n_exp, 1, -1), tm_e, 512)
    ys = _gmm2(tile_expert, n_active, hid, w_down[l], b_down[l].reshape(n_exp, 1, -1), tm_e, 1024)
    out = _combine(pos, ys, x_mid, gate_l, mod3, ln2_g[l], ln2_b[l], _pick(seq, 128), seq)
    return out.reshape(bsz, seq, dm)
```

```python
import functools

import jax
import jax.numpy as jnp
from jax import lax
from jax.experimental import pallas as pl
from jax.experimental.pallas import tpu as pltpu

F32 = jnp.float32
BF16 = jnp.bfloat16
HIGHEST = lax.Precision.HIGHEST

GRID_W = 64
CONV_WIDTH = 31
HEAD_SIZE = 64
D_LORA = 64
D_GATE_LORA = 160
N_DIR = 2
GN_EPS = 64e-5
LN_EPS = 1e-5
TOP_K = 4
SWIGLU_LIMIT = 7.0
SWIGLU_ALPHA = 1.702
DEPTH = 1
DEEPNORM_ALPHA = (2.0 * DEPTH) ** 0.25

LANES = 128
SUBLANES = 8
ZR_PAD = 4096
VMEM_LIMIT = 56 * 1024 * 1024
EXPERT_ROW_TILE = 512
ISSUE_UNROLL = 8


def _cparams(sem):
    return pltpu.CompilerParams(dimension_semantics=sem, vmem_limit_bytes=VMEM_LIMIT)


def _ln(x):
    mu = jnp.mean(x, axis=-1, keepdims=True)
    xc = x - mu
    var = jnp.mean(xc * xc, axis=-1, keepdims=True)
    return xc * lax.rsqrt(var + LN_EPS)


def _sigmoid(x):
    return 1.0 / (1.0 + jnp.exp(-x))


def _ada_kernel(c_ref, w_ref, b_ref, o_ref):
    c = c_ref[...]
    s = c * _sigmoid(c)
    o_ref[...] = jnp.dot(s, w_ref[...], preferred_element_type=F32, precision=HIGHEST) + b_ref[...]


def _ada(c_rows, w, b):
    rows, d = c_rows.shape
    n = w.shape[1]
    tn = 1024
    return pl.pallas_call(
        _ada_kernel,
        grid=(n // tn,),
        in_specs=[pl.BlockSpec((rows, d), lambda j: (0, 0)),
                  pl.BlockSpec((d, tn), lambda j: (0, j)),
                  pl.BlockSpec((1, tn), lambda j: (0, j))],
        out_specs=pl.BlockSpec((rows, tn), lambda j: (0, j)),
        out_shape=jax.ShapeDtypeStruct((rows, n), F32),
        compiler_params=_cparams(("arbitrary",)),
        name="ada",
    )(c_rows, w, b.reshape(1, n))


def _lnmod_mm_kernel(x_ref, sh_ref, sc_ref, w_ref, b_ref, o_ref, xn_ref):
    @pl.when(pl.program_id(1) == 0)
    def _():
        xn = _ln(x_ref[...]) * (1.0 + sc_ref[0]) + sh_ref[0]
        xn_ref[...] = xn.astype(BF16)

    o_ref[...] = jnp.dot(xn_ref[...], w_ref[...], preferred_element_type=F32) + b_ref[...]


def _lnmod_mm(x2, mod3, row_of_tile, shift_slot, w_bf16, bias, tm, tn):
    m_rows, d = x2.shape
    n = w_bf16.shape[1]
    return pl.pallas_call(
        _lnmod_mm_kernel,
        grid=(m_rows // tm, n // tn),
        in_specs=[pl.BlockSpec((tm, d), lambda m, j: (m, 0)),
                  pl.BlockSpec((1, 1, d), lambda m, j: (row_of_tile(m) * 6 + shift_slot, 0, 0)),
                  pl.BlockSpec((1, 1, d), lambda m, j: (row_of_tile(m) * 6 + shift_slot + 1, 0, 0)),
                  pl.BlockSpec((d, tn), lambda m, j: (0, j)),
                  pl.BlockSpec((1, tn), lambda m, j: (0, j))],
        out_specs=pl.BlockSpec((tm, tn), lambda m, j: (m, j)),
        out_shape=jax.ShapeDtypeStruct((m_rows, n), F32),
        scratch_shapes=[pltpu.VMEM((tm, d), BF16)],
        compiler_params=_cparams(("parallel", "arbitrary")),
        name="lnmod_mm",
    )(x2, mod3, mod3, w_bf16, bias.reshape(1, n))


def _prep_kernel(z_ref, zp_ref, zn_ref, mu_ref, w2_ref, a2_ref, g2_ref, w0_ref, a0_ref,
                 r_ref, k_ref, v_ref, a_ref, w_ref, g_ref, *, d_rwkv):
    i = pl.program_id(1)
    nt = pl.num_programs(1)
    z = z_ref[0]
    tt = z.shape[0]
    rows = lax.broadcasted_iota(jnp.int32, (tt, 1), 0)
    halo_p = jnp.where(i == 0, 0.0, zp_ref[0, 7:8, :])
    halo_n = jnp.where(i == nt - 1, 0.0, zn_ref[0, 0:1, :])
    prev = jnp.where(rows == 0, halo_p, pltpu.roll(z, 1, 0))
    nxt = jnp.where(rows == tt - 1, halo_n, pltpu.roll(z, tt - 1, 0))
    zs = z + mu_ref[0:1, :] * (prev - z) + mu_ref[1:2, :] * (nxt - z)
    d = d_rwkv
    r_ref[0] = zs[:, 0:d]
    k_ref[0] = zs[:, d:2 * d]
    v_ref[0] = zs[:, 2 * d:3 * d]
    wd = zs[:, 3 * d:3 * d + LANES]
    ad = zs[:, 3 * d + LANES:3 * d + 2 * LANES]
    gd = zs[:, 3 * d + 2 * LANES:3 * d + 4 * LANES]
    g_ref[0] = jnp.dot(_sigmoid(gd).astype(BF16), g2_ref[...], preferred_element_type=F32)
    lw = jnp.dot(jnp.tanh(wd).astype(BF16), w2_ref[...], preferred_element_type=F32)
    la = jnp.dot(ad.astype(BF16), a2_ref[...], preferred_element_type=F32)
    for dr in range(N_DIR):
        y = -(w0_ref[dr:dr + 1, :] + lw[:, dr * d:(dr + 1) * d])
        softplus = jnp.maximum(y, 0.0) + jnp.log(1.0 + jnp.exp(-jnp.abs(y)))
        w_log = -softplus - 0.5
        w_ref[dr, 0] = jnp.exp(-jnp.exp(w_log))
        a_ref[dr, 0] = _sigmoid(a0_ref[dr:dr + 1, :] + la[:, dr * d:(dr + 1) * d])


def _prep(z3, mu_p, w2blk, a2blk, g2p, w0, a0, tt, d_rwkv):
    b, t, _ = z3.shape
    nt = t // tt
    tb = tt // 8
    nat = jax.ShapeDtypeStruct((b, t, d_rwkv), F32)
    nat2 = jax.ShapeDtypeStruct((N_DIR, b, t, d_rwkv), F32)
    full = lambda a: pl.BlockSpec(a.shape, lambda bi, i: (0,) * a.ndim)
    o1 = pl.BlockSpec((1, tt, d_rwkv), lambda bi, i: (bi, i, 0))
    o2 = pl.BlockSpec((N_DIR, 1, tt, d_rwkv), lambda bi, i: (0, bi, i, 0))
    return pl.pallas_call(
        functools.partial(_prep_kernel, d_rwkv=d_rwkv),
        grid=(b, nt),
        in_specs=[pl.BlockSpec((1, tt, ZR_PAD), lambda bi, i: (bi, i, 0)),
                  pl.BlockSpec((1, 8, ZR_PAD), lambda bi, i: (bi, jnp.maximum(i * tb - 1, 0), 0)),
                  pl.BlockSpec((1, 8, ZR_PAD), lambda bi, i: (bi, jnp.minimum((i + 1) * tb, t // 8 - 1), 0)),
                  full(mu_p), full(w2blk), full(a2blk), full(g2p), full(w0), full(a0)],
        out_specs=[o1, o1, o1, o2, o2, o1],
        out_shape=[nat, nat, nat, nat2, nat2, nat],
        compiler_params=_cparams(("parallel", "arbitrary")),
        name="prep",
    )(z3, z3, z3, mu_p, w2blk, a2blk, g2p, w0, a0)


def _scan_kernel(rc_ref, kc_ref, vc_ref, ac_ref, wc_ref, rl_ref, kl_ref, vl_ref, al_ref, wl_ref,
                 kk_ref, ka_ref, y_ref, s_ref, ops_ref, *, tt, n, nb_ctx):
    d = pl.program_id(0)
    i = pl.program_id(1)
    nvb = n // SUBLANES

    @pl.when(i == 0)
    def _():
        s_ref[...] = jnp.zeros_like(s_ref)

    k_k = kk_ref[...]
    k_a = ka_ref[...]

    def stage_operands(t, r_ref, k_ref, v_ref, a_ref, w_ref):
        k = k_ref[t]
        a = a_ref[0, t]
        kk = k * k_k
        nrm = jnp.sqrt(jnp.sum(kk * kk, axis=0, keepdims=True))
        kk = kk / jnp.maximum(nrm, 1e-12)
        ops_ref[0] = kk
        ops_ref[1] = w_ref[0, t]
        ops_ref[2] = -(kk * a)
        ops_ref[3] = k * (1.0 + (a - 1.0) * k_a)
        ops_ref[4] = r_ref[t]
        ops_ref[5] = v_ref[t]

    def step(j, carry):
        t = jnp.where(d == 0, j, tt - 1 - j)

        @pl.when(i < nb_ctx)
        def _():
            stage_operands(t, rc_ref, kc_ref, vc_ref, ac_ref, wc_ref)

        @pl.when(i >= nb_ctx)
        def _():
            stage_operands(t, rl_ref, kl_ref, vl_ref, al_ref, wl_ref)

        row = lambda o, kx: ops_ref[o, pl.ds(kx, 1), :]
        tile = lambda kx, vb: s_ref[kx, pl.ds(vb * SUBLANES, SUBLANES), :]
        acc = [[None, None] for _ in range(nvb)]
        for kx in range(n):
            kb = row(0, kx)
            for vb in range(nvb):
                p = tile(kx, vb) * kb
                cur = acc[vb][kx % 2]
                acc[vb][kx % 2] = p if cur is None else cur + p
        skk = [acc[vb][0] + acc[vb][1] for vb in range(nvb)]
        vv = [ops_ref[5, pl.ds(vb * SUBLANES, SUBLANES), :] for vb in range(nvb)]
        yacc = [None] * nvb
        for kx in range(n):
            wb, nb_, kdb, rb = row(1, kx), row(2, kx), row(3, kx), row(4, kx)
            for vb in range(nvb):
                sn = tile(kx, vb) * wb + skk[vb] * nb_ + vv[vb] * kdb
                s_ref[kx, pl.ds(vb * SUBLANES, SUBLANES), :] = sn
                q = sn * rb
                yacc[vb] = q if yacc[vb] is None else yacc[vb] + q
        for vb in range(nvb):
            y_ref[0, t, pl.ds(vb * SUBLANES, SUBLANES), :] = yacc[vb]
        return carry

    lax.fori_loop(0, tt, step, 0)


def _scan(ctx_ops, lat_ops, kk_t, ka_t, tt):
    t_ctx, n, c = ctx_ops[0].shape
    t_lat = lat_ops[0].shape[0]
    nb_ctx, nb_lat = t_ctx // tt, t_lat // tt

    def blk_ctx(d, i):
        j = jnp.minimum(i, nb_ctx - 1)
        return jnp.where(d == 0, j, nb_ctx - 1 - j)

    def blk_lat(d, i):
        j = jnp.maximum(i - nb_ctx, 0)
        return jnp.where(d == 0, j, nb_lat - 1 - j)

    def specs(blk):
        s1 = pl.BlockSpec((tt, n, c), lambda d, i: (blk(d, i), 0, 0))
        s2 = pl.BlockSpec((1, tt, n, c), lambda d, i: (d, blk(d, i), 0, 0))
        return [s1, s1, s1, s2, s2], s2

    ctx_specs, _ = specs(blk_ctx)
    lat_specs, y_spec = specs(blk_lat)
    const = pl.BlockSpec((n, c), lambda d, i: (0, 0))
    return pl.pallas_call(
        functools.partial(_scan_kernel, tt=tt, n=n, nb_ctx=nb_ctx),
        grid=(N_DIR, nb_ctx + nb_lat),
        in_specs=ctx_specs + lat_specs + [const, const],
        out_specs=y_spec,
        out_shape=jax.ShapeDtypeStruct((N_DIR, t_lat, n, c), F32),
        scratch_shapes=[pltpu.VMEM((n, n, c), F32), pltpu.VMEM((6, n, c), F32)],
        compiler_params=_cparams(("arbitrary", "arbitrary")),
        name="scan",
    )(*ctx_ops, *lat_ops, kk_t, ka_t)


def _post_kernel(y_ref, r_ref, k_ref, v_ref, a_ref, ka_ref, rk_ref, g_ref, b_ref, o_ref):
    y = y_ref[0] + y_ref[1]
    mu = jnp.mean(y, axis=1, keepdims=True)
    yc = y - mu
    var = jnp.mean(yc * yc, axis=1, keepdims=True)
    yn = yc * lax.rsqrt(var + GN_EPS) * g_ref[...] + b_ref[...]
    r = r_ref[...]
    k = k_ref[...]
    k_a = ka_ref[...]
    rk = 0.0
    for dr in range(N_DIR):
        kd = k * (1.0 + (a_ref[dr] - 1.0) * k_a)
        rk = rk + jnp.sum(r * kd * rk_ref[...], axis=1, keepdims=True)
    o_ref[...] = yn + rk * v_ref[...]


def _post(y, r, k, v, a, ka_t, rk_t, g_t, b_t, tt, t_off):
    _, t_tot, n, c = y.shape
    ob = t_off // tt
    nb = (t_tot - t_off) // tt
    s1 = pl.BlockSpec((tt, n, c), lambda i: (i + ob, 0, 0))
    s2 = pl.BlockSpec((N_DIR, tt, n, c), lambda i: (0, i + ob, 0, 0))
    const = pl.BlockSpec((n, c), lambda i: (0, 0))
    return pl.pallas_call(
        _post_kernel,
        grid=(nb,),
        in_specs=[s2, s1, s1, s1, s2, const, const, const, const],
        out_specs=pl.BlockSpec((tt, n, c), lambda i: (i, 0, 0)),
        out_shape=jax.ShapeDtypeStruct((t_tot - t_off, n, c), F32),
        compiler_params=_cparams(("parallel",)),
        name="post",
    )(y, r, k, v, a, ka_t, rk_t, g_t, b_t)


def _conv_kernel(za_ref, zg_ref, cw_ref, cb_ref, lg_ref, lb_ref, wo_ref, bo_ref, o_ref):
    h = za_ref[...] * _sigmoid(zg_ref[...])
    tm = h.shape[0]
    pos = lax.broadcasted_iota(jnp.int32, (tm, 1), 0) % GRID_W
    half = CONV_WIDTH // 2
    acc = jnp.zeros_like(h)
    for j in range(CONV_WIDTH):
        off = j - half
        sh = pltpu.roll(h, (-off) % tm, 0) if off != 0 else h
        ok = jnp.logical_and(pos + off >= 0, pos + off < GRID_W)
        acc = acc + jnp.where(ok, sh, 0.0) * cw_ref[j:j + 1, :]
    acc = acc + cb_ref[...]
    hn = _ln(acc) * lg_ref[...] + lb_ref[...]
    act = hn * _sigmoid(hn)
    o_ref[...] = jnp.dot(act.astype(BF16), wo_ref[...], preferred_element_type=F32) + bo_ref[...]


def _conv(z, col_a, col_g, cw, cb, lg, lb, wo_bf16, bo, tm):
    n_rows = z.shape[0]
    dc = cw.shape[1]
    dm = wo_bf16.shape[1]
    full = lambda a: pl.BlockSpec(a.shape, lambda m: (0,) * a.ndim)
    cb, lg, lb, bo = cb.reshape(1, dc), lg.reshape(1, dc), lb.reshape(1, dc), bo.reshape(1, dm)
    return pl.pallas_call(
        _conv_kernel,
        grid=(n_rows // tm,),
        in_specs=[pl.BlockSpec((tm, dc), lambda m: (m, col_a)),
                  pl.BlockSpec((tm, dc), lambda m: (m, col_g)),
                  full(cw), full(cb), full(lg), full(lb), full(wo_bf16), full(bo)],
        out_specs=pl.BlockSpec((tm, dm), lambda m: (m, 0)),
        out_shape=jax.ShapeDtypeStruct((n_rows, dm), F32),
        compiler_params=_cparams(("parallel",)),
        name="conv",
    )(z, z, cw, cb, lg, lb, wo_bf16, bo)


def _merge_kernel(o_ref, g_ref, yc_ref, z1_ref, z2_ref, x_ref, gate_ref, wr_ref, wo_ref, bo_ref,
                  lg_ref, lb_ref, out_ref):
    y_rwkv = jnp.dot((o_ref[...] * g_ref[...]).astype(BF16), wr_ref[...], preferred_element_type=F32)
    mix = _sigmoid(z1_ref[...]) * yc_ref[...] + _sigmoid(z2_ref[...]) * y_rwkv
    out = jnp.dot(mix.astype(BF16), wo_ref[...], preferred_element_type=F32) + bo_ref[...]
    h = DEEPNORM_ALPHA * x_ref[...] + gate_ref[0] * out
    out_ref[...] = _ln(h) * lg_ref[...] + lb_ref[...]


def _merge(o_nat, g_nat, y_conv, z, col_g1, x2, mod3, wr_bf16, wo_bf16, bo, lg, lb, tm, rows_per_batch):
    n_rows, dm = x2.shape
    dr = o_nat.shape[1]
    full = lambda a: pl.BlockSpec(a.shape, lambda m: (0,) * a.ndim)
    bo, lg, lb = bo.reshape(1, dm), lg.reshape(1, dm), lb.reshape(1, dm)
    tiles_per_batch = rows_per_batch // tm
    return pl.pallas_call(
        _merge_kernel,
        grid=(n_rows // tm,),
        in_specs=[pl.BlockSpec((tm, dr), lambda m: (m, 0)),
                  pl.BlockSpec((tm, dr), lambda m: (m, 0)),
                  pl.BlockSpec((tm, dm), lambda m: (m, 0)),
                  pl.BlockSpec((tm, dm), lambda m: (m, col_g1)),
                  pl.BlockSpec((tm, dm), lambda m: (m, col_g1 + 1)),
                  pl.BlockSpec((tm, dm), lambda m: (m, 0)),
                  pl.BlockSpec((1, 1, dm), lambda m: ((m // tiles_per_batch) * 6 + 2, 0, 0)),
                  full(wr_bf16), full(wo_bf16), full(bo), full(lg), full(lb)],
        out_specs=pl.BlockSpec((tm, dm), lambda m: (m, 0)),
        out_shape=jax.ShapeDtypeStruct((n_rows, dm), F32),
        compiler_params=_cparams(("parallel",)),
        name="merge",
    )(o_nat, g_nat, y_conv, z, z, x2, mod3, wr_bf16, wo_bf16, bo, lg, lb)


def _router_kernel(x_ref, sh_ref, sc_ref, wr_ref, br_ref, u_ref, idx_ref, gate_ref):
    u = _ln(x_ref[...]) * (1.0 + sc_ref[0]) + sh_ref[0]
    u_ref[...] = u
    logits = jnp.dot(u, wr_ref[...], preferred_element_type=F32, precision=HIGHEST) + br_ref[...]
    lane = lax.broadcasted_iota(jnp.int32, logits.shape, 1)
    vals, idxs = [], []
    l = logits
    for _ in range(TOP_K):
        m = jnp.max(l, axis=-1, keepdims=True)
        ix = jnp.min(jnp.where(l == m, lane, LANES), axis=-1, keepdims=True)
        vals.append(m)
        idxs.append(ix)
        l = jnp.where(lane == ix, -jnp.inf, l)
    es = [jnp.exp(vv - vals[0]) for vv in vals]
    den = es[0] + es[1] + es[2] + es[3]
    idx_out = jnp.zeros(logits.shape, jnp.int32)
    gate_out = jnp.zeros(logits.shape, F32)
    for j in range(TOP_K):
        idx_out = jnp.where(lane == j, idxs[j], idx_out)
        gate_out = jnp.where(lane == j, es[j] / den, gate_out)
    idx_ref[...] = idx_out
    gate_ref[...] = gate_out


def _router(x_mid, mod3, w_router_p, b_router_p, tm, rows_per_batch):
    n_rows, dm = x_mid.shape
    tiles_per_batch = rows_per_batch // tm
    full = lambda a: pl.BlockSpec(a.shape, lambda m: (0,) * a.ndim)
    row = pl.BlockSpec((tm, dm), lambda m: (m, 0))
    lane_out = pl.BlockSpec((tm, LANES), lambda m: (m, 0))
    return pl.pallas_call(
        _router_kernel,
        grid=(n_rows // tm,),
        in_specs=[row,
                  pl.BlockSpec((1, 1, dm), lambda m: ((m // tiles_per_batch) * 6 + 3, 0, 0)),
                  pl.BlockSpec((1, 1, dm), lambda m: ((m // tiles_per_batch) * 6 + 4, 0, 0)),
                  full(w_router_p), full(b_router_p)],
        out_specs=[row, lane_out, lane_out],
        out_shape=[jax.ShapeDtypeStruct((n_rows, dm), F32),
                   jax.ShapeDtypeStruct((n_rows, LANES), jnp.int32),
                   jax.ShapeDtypeStruct((n_rows, LANES), F32)],
        compiler_params=_cparams(("parallel",)),
        name="router",
    )(x_mid, mod3, mod3, w_router_p, b_router_p)


def _group_changed(te_ref, m):
    return jnp.logical_or(m == 0, te_ref[m] != te_ref[jnp.maximum(m - 1, 0)])


def _gmm1_kernel(te_ref, na_ref, x_ref, wg_ref, wu_ref, bg_ref, bu_ref, o_ref, wgb_ref, wub_ref):
    m = pl.program_id(1)

    @pl.when(_group_changed(te_ref, m))
    def _():
        wgb_ref[...] = wg_ref[0].astype(BF16)
        wub_ref[...] = wu_ref[0].astype(BF16)

    @pl.when(m < na_ref[0])
    def _():
        x = x_ref[...]
        gate = jnp.dot(x, wgb_ref[...], preferred_element_type=F32) + bg_ref[0]
        up = jnp.dot(x, wub_ref[...], preferred_element_type=F32) + bu_ref[0]
        gate = jnp.minimum(gate, SWIGLU_LIMIT)
        up = jnp.clip(up, -SWIGLU_LIMIT, SWIGLU_LIMIT)
        hid = (up + 1.0) * (gate * _sigmoid(SWIGLU_ALPHA * gate))
        o_ref[...] = hid.astype(BF16)

    @pl.when(m >= na_ref[0])
    def _():
        o_ref[...] = jnp.zeros_like(o_ref)


def _active_tile(m, na):
    return jnp.minimum(m, na[0] - 1)


def _gmm1(tile_expert, n_active, xs, w_gate_up, b_gate_up3, tm, th):
    p_pad, d = xs.shape
    de = w_gate_up.shape[2] // 2
    nj = de // th
    grid_spec = pltpu.PrefetchScalarGridSpec(
        num_scalar_prefetch=2,
        grid=(nj, p_pad // tm),
        in_specs=[pl.BlockSpec((tm, d), lambda j, m, te, na: (_active_tile(m, na), 0)),
                  pl.BlockSpec((1, d, th), lambda j, m, te, na: (te[m], 0, j)),
                  pl.BlockSpec((1, d, th), lambda j, m, te, na: (te[m], 0, j + nj)),
                  pl.BlockSpec((1, 1, th), lambda j, m, te, na: (te[m], 0, j)),
                  pl.BlockSpec((1, 1, th), lambda j, m, te, na: (te[m], 0, j + nj))],
        out_specs=pl.BlockSpec((tm, th), lambda j, m, te, na: (m, j)),
        scratch_shapes=[pltpu.VMEM((d, th), BF16), pltpu.VMEM((d, th), BF16)],
    )
    return pl.pallas_call(
        _gmm1_kernel,
        grid_spec=grid_spec,
        out_shape=jax.ShapeDtypeStruct((p_pad, de), BF16),
        compiler_params=_cparams(("arbitrary", "arbitrary")),
        name="gmm1",
    )(tile_expert, n_active, xs, w_gate_up, w_gate_up, b_gate_up3, b_gate_up3)


def _gmm2_kernel(te_ref, na_ref, h_ref, w_ref, b_ref, o_ref, wb_ref):
    m = pl.program_id(1)

    @pl.when(_group_changed(te_ref, m))
    def _():
        wb_ref[...] = w_ref[0].astype(BF16)

    @pl.when(m < na_ref[0])
    def _():
        o_ref[...] = jnp.dot(h_ref[...], wb_ref[...], preferred_element_type=F32) + b_ref[0]

    @pl.when(m >= na_ref[0])
    def _():
        o_ref[...] = jnp.zeros_like(o_ref)


def _gmm2(tile_expert, n_active, hid, w_down, b_down3, tm, tn):
    p_pad, de = hid.shape
    d = w_down.shape[2]
    grid_spec = pltpu.PrefetchScalarGridSpec(
        num_scalar_prefetch=2,
        grid=(d // tn, p_pad // tm),
        in_specs=[pl.BlockSpec((tm, de), lambda j, m, te, na: (_active_tile(m, na), 0)),
                  pl.BlockSpec((1, de, tn), lambda j, m, te, na: (te[m], 0, j)),
                  pl.BlockSpec((1, 1, tn), lambda j, m, te, na: (te[m], 0, j))],
        out_specs=pl.BlockSpec((tm, tn), lambda j, m, te, na: (m, j)),
        scratch_shapes=[pltpu.VMEM((de, tn), BF16)],
    )
    return pl.pallas_call(
        _gmm2_kernel,
        grid_spec=grid_spec,
        out_shape=jax.ShapeDtypeStruct((p_pad, d), F32),
        compiler_params=_cparams(("arbitrary", "arbitrary")),
        name="gmm2",
    )(tile_expert, n_active, hid, w_down, b_down3)


def _row_copy(src_hbm, src_row, buf_ref, slot, dst_row, sem_ref):
    return pltpu.make_async_copy(src_hbm.at[pl.ds(src_row, 1), :],
                                 buf_ref.at[slot, pl.ds(dst_row, 1), :], sem_ref.at[slot])


def _wait_rows(src_hbm, buf_ref, slot, sem_ref):
    n = buf_ref.shape[1]
    pltpu.make_async_copy(src_hbm.at[pl.ds(0, n), :], buf_ref.at[slot], sem_ref.at[slot]).wait()


def _dispatch_kernel(rt_ref, na_ref, u_hbm, o_ref, buf_ref, sem_ref, *, tm):
    i = pl.program_id(0)
    na = na_ref[0]

    def issue(tile, slot):
        def body(r, carry):
            _row_copy(u_hbm, rt_ref[tile * tm + r], buf_ref, slot, r, sem_ref).start()
            return carry
        lax.fori_loop(0, tm, body, 0, unroll=ISSUE_UNROLL)

    @pl.when(jnp.logical_and(i == 0, na > 0))
    def _():
        issue(0, 0)

    @pl.when(i + 1 < na)
    def _():
        issue(i + 1, (i + 1) % 2)

    @pl.when(i < na)
    def _():
        slot = i % 2
        _wait_rows(u_hbm, buf_ref, slot, sem_ref)
        o_ref[...] = buf_ref[slot].astype(BF16)

    @pl.when(i >= na)
    def _():
        o_ref[...] = jnp.zeros_like(o_ref)


def _dispatch(row_token, n_active, u_f, tm):
    p_pad = row_token.shape[0]
    dm = u_f.shape[1]
    grid_spec = pltpu.PrefetchScalarGridSpec(
        num_scalar_prefetch=2,
        grid=(p_pad // tm,),
        in_specs=[pl.BlockSpec(memory_space=pl.ANY)],
        out_specs=pl.BlockSpec((tm, dm), lambda i, rt, na: (i, 0)),
        scratch_shapes=[pltpu.VMEM((2, tm, dm), F32), pltpu.SemaphoreType.DMA((2,))],
    )
    return pl.pallas_call(
        functools.partial(_dispatch_kernel, tm=tm),
        grid_spec=grid_spec,
        out_shape=jax.ShapeDtypeStruct((p_pad, dm), BF16),
        compiler_params=_cparams(("arbitrary",)),
        name="dispatch",
    )(row_token, n_active, u_f)


def _combine_kernel(pos_ref, ys_hbm, x_ref, gl_ref, gate_ref, lg_ref, lb_ref, o_ref, buf_ref, sem_ref, *, tm):
    i = pl.program_id(0)
    n = pl.num_programs(0)

    def issue(tile, slot):
        def body(r, carry):
            for j in range(TOP_K):
                p = pos_ref[(tile * tm + r) * TOP_K + j]
                _row_copy(ys_hbm, p, buf_ref, slot, j * tm + r, sem_ref).start()
            return carry
        lax.fori_loop(0, tm, body, 0, unroll=ISSUE_UNROLL // TOP_K)

    @pl.when(i == 0)
    def _():
        issue(0, 0)

    @pl.when(i + 1 < n)
    def _():
        issue(i + 1, (i + 1) % 2)

    slot = i % 2
    _wait_rows(ys_hbm, buf_ref, slot, sem_ref)
    gl = gl_ref[...]
    moe = gl[:, 0:1] * buf_ref[slot, pl.ds(0, tm), :]
    for j in range(1, TOP_K):
        moe = moe + gl[:, j:j + 1] * buf_ref[slot, pl.ds(j * tm, tm), :]
    h = DEEPNORM_ALPHA * x_ref[...] + gate_ref[0] * moe
    o_ref[...] = _ln(h) * lg_ref[...] + lb_ref[...]


def _combine(pos, ys, x_mid, gate_l, mod3, lg, lb, tm, rows_per_batch):
    n_rows, dm = x_mid.shape
    tiles_per_batch = rows_per_batch // tm
    row = pl.BlockSpec((tm, dm), lambda m, p: (m, 0))
    vec = pl.BlockSpec((1, dm), lambda m, p: (0, 0))
    grid_spec = pltpu.PrefetchScalarGridSpec(
        num_scalar_prefetch=1,
        grid=(n_rows // tm,),
        in_specs=[pl.BlockSpec(memory_space=pl.ANY), row,
                  pl.BlockSpec((tm, LANES), lambda m, p: (m, 0)),
                  pl.BlockSpec((1, 1, dm), lambda m, p: ((m // tiles_per_batch) * 6 + 5, 0, 0)),
                  vec, vec],
        out_specs=row,
        scratch_shapes=[pltpu.VMEM((2, TOP_K * tm, dm), F32), pltpu.SemaphoreType.DMA((2,))],
    )
    return pl.pallas_call(
        functools.partial(_combine_kernel, tm=tm),
        grid_spec=grid_spec,
        out_shape=jax.ShapeDtypeStruct((n_rows, dm), F32),
        compiler_params=_cparams(("arbitrary",)),
        name="combine",
    )(pos, ys, x_mid, gate_l, mod3, lg.reshape(1, dm), lb.reshape(1, dm))


def _to_scan_layout(a, n_heads):
    *lead, b, t, _ = a.shape
    nl = len(lead)
    a = a.reshape(*lead, b, t, n_heads, HEAD_SIZE)
    perm = tuple(range(nl)) + (nl + 1, nl + 3, nl + 0, nl + 2)
    return a.transpose(perm).reshape(*lead, t, HEAD_SIZE, b * n_heads)


def _head_tile(p, n_heads, batch):
    return jnp.tile(p.reshape(n_heads, HEAD_SIZE).T, (1, batch))


def _pick(n, pref):
    t = min(n, pref)
    while n % t:
        t //= 2
    return t


def kernel(x, c, ctx, c_ctx, w_ada, b_ada, w_in, b_in, shift_mu, conv_w, conv_b, conv_ln_g, conv_ln_b,
           w_conv_o, b_conv_o, w0, w2, a0, a2, g2, k_k, k_a, r_k, lnx_g, lnx_b, w_rwkv_o, w_out, b_out,
           ln1_g, ln1_b, w_router, b_router, w_gate_up, b_gate_up, w_down, b_down, ln2_g, ln2_b):
    bsz, seq, dm = x.shape
    n_ctx = ctx.shape[1]
    dc = conv_w.shape[2]
    dr = k_k.shape[1]
    n_heads = dr // HEAD_SIZE
    d_shift = shift_mu.shape[2]
    n_exp = w_router.shape[2]
    l = 0
    n_tok = bsz * seq

    c_rows = jnp.zeros((16, dm), F32).at[:bsz].set(c).at[bsz].set(c_ctx)
    mod = _ada(c_rows, w_ada[l], b_ada[l])
    mod3 = mod.reshape(16 * 6, 1, dm)

    pad = ZR_PAD - d_shift
    wl = w_in[l]
    w_p = jnp.concatenate([wl[:, 2 * dc:2 * dc + d_shift], jnp.zeros((dm, pad), F32),
                           wl[:, :2 * dc], wl[:, 2 * dc + d_shift:]], axis=1).astype(BF16)
    bl = b_in[l]
    b_p = jnp.concatenate([bl[2 * dc:2 * dc + d_shift], jnp.zeros((pad,), F32),
                           bl[:2 * dc], bl[2 * dc + d_shift:]])
    x2 = x.reshape(n_tok, dm)
    tm_in = _pick(seq, 1024)
    z_l = _lnmod_mm(x2, mod3, lambda m: (m * tm_in) // seq, 0, w_p, b_p, tm_in, 512)
    tm_c = _pick(n_ctx, 1024)
    z_c = _lnmod_mm(ctx.reshape(bsz * n_ctx, dm), mod3, lambda m: bsz, 0,
                    w_p[:, :ZR_PAD], b_p[:ZR_PAD], tm_c, 512)

    mu_p = jnp.pad(shift_mu[l], ((0, 0), (0, pad)))
    w2blk = jnp.zeros((LANES, N_DIR * dr), F32)
    a2blk = jnp.zeros((LANES, N_DIR * dr), F32)
    for d in range(N_DIR):
        w2blk = w2blk.at[d * D_LORA:(d + 1) * D_LORA, d * dr:(d + 1) * dr].set(w2[l, d])
        a2blk = a2blk.at[d * D_LORA:(d + 1) * D_LORA, d * dr:(d + 1) * dr].set(a2[l, d])
    g2p = jnp.zeros((2 * LANES, dr), F32).at[:D_GATE_LORA].set(g2[l])
    prep_args = (mu_p, w2blk.astype(BF16), a2blk.astype(BF16), g2p.astype(BF16), w0[l], a0[l])
    tt_p = _pick(seq, 256)
    r_l, k_l, v_l, a_l, w_l, g_l = _prep(z_l.reshape(bsz, seq, -1), *prep_args, tt_p, dr)
    r_c, k_c, v_c, a_c, w_c, _ = _prep(z_c.reshape(bsz, n_ctx, -1), *prep_args, _pick(n_ctx, 256), dr)

    ctx_ops = tuple(_to_scan_layout(t, n_heads) for t in (r_c, k_c, v_c, a_c, w_c))
    lat_ops = tuple(_to_scan_layout(t, n_heads) for t in (r_l, k_l, v_l, a_l, w_l))
    ka_t = _head_tile(k_a[l], n_heads, bsz)
    tt_s = _pick(n_ctx, 32)
    y_s = _scan(ctx_ops, lat_ops, _head_tile(k_k[l], n_heads, bsz), ka_t, tt_s)
    o_s = _post(y_s, *lat_ops[:4], ka_t, jnp.tile(r_k[l].T, (1, bsz)),
                _head_tile(lnx_g[l], n_heads, bsz), _head_tile(lnx_b[l], n_heads, bsz), tt_s, 0)
    o_nat = o_s.reshape(seq, HEAD_SIZE, bsz, n_heads).transpose(2, 0, 3, 1).reshape(n_tok, dr)

    zcol = ZR_PAD // dc
    y_conv = _conv(z_l, zcol, zcol + 1, conv_w[l], conv_b[l], conv_ln_g[l], conv_ln_b[l],
                   w_conv_o[l].astype(BF16), b_conv_o[l], _pick(seq, 512))
    gcol = (ZR_PAD + 2 * dc) // dm
    tm_m = _pick(seq, 128)
    x_mid = _merge(o_nat, g_l.reshape(n_tok, dr), y_conv, z_l, gcol, x2, mod3, w_rwkv_o[l].astype(BF16),
                   w_out[l].astype(BF16), b_out[l], ln1_g[l], ln1_b[l], tm_m, seq)

    w_r_p = jnp.zeros((dm, LANES), F32).at[:, :n_exp].set(w_router[l])
    b_r_p = jnp.full((1, LANES), -1e30, F32).at[0, :n_exp].set(b_router[l])
    u_f, idx_l, gate_l = _router(x_mid, mod3, w_r_p, b_r_p, tm_m, seq)
    top_idx = idx_l[:, :TOP_K]

    tm_e = EXPERT_ROW_TILE
    n_pair = n_tok * TOP_K
    n_tiles = -(-n_pair // tm_e) + n_exp
    p_pad = n_tiles * tm_e
    eq = top_idx[:, :, None] == jnp.arange(n_exp, dtype=jnp.int32)[None, None, :]
    onehot = jnp.sum(eq, axis=1, dtype=jnp.int32)
    cnt_incl = jnp.cumsum(onehot, axis=0)
    counts = cnt_incl[-1]
    cnt_excl = cnt_incl - onehot
    padded = ((counts + tm_e - 1) // tm_e) * tm_e
    pend = jnp.cumsum(padded)
    pstart = pend - padded
    pos = jnp.sum(jnp.where(eq, (cnt_excl + pstart[None, :])[:, None, :], 0), axis=2).reshape(-1)
    row_token = jnp.zeros((p_pad,), jnp.int32).at[pos].set(jnp.arange(n_pair, dtype=jnp.int32) // TOP_K)
    n_active = (pend[-1] // tm_e).astype(jnp.int32).reshape(1)
    tile_start = jnp.arange(n_tiles, dtype=jnp.int32) * tm_e
    tile_expert = jnp.sum(pend[None, :] <= tile_start[:, None], axis=1, dtype=jnp.int32)
    last_e = jnp.sum(pend <= pend[-1] - 1, dtype=jnp.int32)
    tile_expert = jnp.minimum(tile_expert, last_e)

    xs = _dispatch(row_token, n_active, u_f, tm_e)
    hid = _gmm1(tile_expert, n_active, xs, w_gate_up[l], b_gate_up[l].reshape(n_exp, 1, -1), tm_e, 1024)
    ys = _gmm2(tile_expert, n_active, hid, w_down[l], b_down[l].reshape(n_exp, 1, -1), tm_e, 1024)
    out = _combine(pos, ys, x_mid, gate_l, mod3, ln2_g[l], ln2_b[l], _pick(seq, 128), seq)
    return out.reshape(bsz, seq, dm)
```

```python
import functools

import jax
import jax.numpy as jnp
from jax import lax
from jax.experimental import pallas as pl
from jax.experimental.pallas import tpu as pltpu

F32 = jnp.float32
BF16 = jnp.bfloat16
HIGHEST = lax.Precision.HIGHEST

GRID_W = 64
CONV_WIDTH = 31
HEAD_SIZE = 64
D_LORA = 64
D_GATE_LORA = 160
N_DIR = 2
GN_EPS = 64e-5
LN_EPS = 1e-5
TOP_K = 4
SWIGLU_LIMIT = 7.0
SWIGLU_ALPHA = 1.702
DEPTH = 1
DEEPNORM_ALPHA = (2.0 * DEPTH) ** 0.25

LANES = 128
SUBLANES = 8
ZR_PAD = 4096
VMEM_LIMIT = 56 * 1024 * 1024
EXPERT_ROW_TILE = 512
ISSUE_UNROLL = 8


def _cparams(sem):
    return pltpu.CompilerParams(dimension_semantics=sem, vmem_limit_bytes=VMEM_LIMIT)


def _ln(x):
    mu = jnp.mean(x, axis=-1, keepdims=True)
    xc = x - mu
    var = jnp.mean(xc * xc, axis=-1, keepdims=True)
    return xc * lax.rsqrt(var + LN_EPS)


def _sigmoid(x):
    return 1.0 / (1.0 + jnp.exp(-x))


def _ada_kernel(c_ref, w_ref, b_ref, o_ref):
    c = c_ref[...]
    s = c * _sigmoid(c)
    o_ref[...] = jnp.dot(s, w_ref[...], preferred_element_type=F32, precision=HIGHEST) + b_ref[...]


def _ada(c_rows, w, b):
    rows, d = c_rows.shape
    n = w.shape[1]
    tn = 1024
    return pl.pallas_call(
        _ada_kernel,
        grid=(n // tn,),
        in_specs=[pl.BlockSpec((rows, d), lambda j: (0, 0)),
                  pl.BlockSpec((d, tn), lambda j: (0, j)),
                  pl.BlockSpec((1, tn), lambda j: (0, j))],
        out_specs=pl.BlockSpec((rows, tn), lambda j: (0, j)),
        out_shape=jax.ShapeDtypeStruct((rows, n), F32),
        compiler_params=_cparams(("arbitrary",)),
        name="ada",
    )(c_rows, w, b.reshape(1, n))


def _lnmod_mm_kernel(x_ref, sh_ref, sc_ref, w_ref, b_ref, o_ref, xn_ref):
    @pl.when(pl.program_id(1) == 0)
    def _():
        xn = _ln(x_ref[...]) * (1.0 + sc_ref[0]) + sh_ref[0]
        xn_ref[...] = xn.astype(BF16)

    o_ref[...] = jnp.dot(xn_ref[...], w_ref[...], preferred_element_type=F32) + b_ref[...]


def _lnmod_mm(x2, mod3, row_of_tile, shift_slot, w_bf16, bias, tm, tn):
    m_rows, d = x2.shape
    n = w_bf16.shape[1]
    return pl.pallas_call(
        _lnmod_mm_kernel,
        grid=(m_rows // tm, n // tn),
        in_specs=[pl.BlockSpec((tm, d), lambda m, j: (m, 0)),
                  pl.BlockSpec((1, 1, d), lambda m, j: (row_of_tile(m) * 6 + shift_slot, 0, 0)),
                  pl.BlockSpec((1, 1, d), lambda m, j: (row_of_tile(m) * 6 + shift_slot + 1, 0, 0)),
                  pl.BlockSpec((d, tn), lambda m, j: (0, j)),
                  pl.BlockSpec((1, tn), lambda m, j: (0, j))],
        out_specs=pl.BlockSpec((tm, tn), lambda m, j: (m, j)),
        out_shape=jax.ShapeDtypeStruct((m_rows, n), F32),
        scratch_shapes=[pltpu.VMEM((tm, d), BF16)],
        compiler_params=_cparams(("parallel", "arbitrary")),
        name="lnmod_mm",
    )(x2, mod3, mod3, w_bf16, bias.reshape(1, n))


def _prep_kernel(z_ref, zp_ref, zn_ref, mu_ref, w2_ref, a2_ref, g2_ref, w0_ref, a0_ref,
                 r_ref, k_ref, v_ref, a_ref, w_ref, g_ref, *, d_rwkv):
    i = pl.program_id(1)
    nt = pl.num_programs(1)
    z = z_ref[0]
    tt = z.shape[0]
    rows = lax.broadcasted_iota(jnp.int32, (tt, 1), 0)
    halo_p = jnp.where(i == 0, 0.0, zp_ref[0, 7:8, :])
    halo_n = jnp.where(i == nt - 1, 0.0, zn_ref[0, 0:1, :])
    prev = jnp.where(rows == 0, halo_p, pltpu.roll(z, 1, 0))
    nxt = jnp.where(rows == tt - 1, halo_n, pltpu.roll(z, tt - 1, 0))
    zs = z + mu_ref[0:1, :] * (prev - z) + mu_ref[1:2, :] * (nxt - z)
    d = d_rwkv
    r_ref[0] = zs[:, 0:d]
    k_ref[0] = zs[:, d:2 * d]
    v_ref[0] = zs[:, 2 * d:3 * d]
    wd = zs[:, 3 * d:3 * d + LANES]
    ad = zs[:, 3 * d + LANES:3 * d + 2 * LANES]
    gd = zs[:, 3 * d + 2 * LANES:3 * d + 4 * LANES]
    g_ref[0] = jnp.dot(_sigmoid(gd).astype(BF16), g2_ref[...], preferred_element_type=F32)
    lw = jnp.dot(jnp.tanh(wd).astype(BF16), w2_ref[...], preferred_element_type=F32)
    la = jnp.dot(ad.astype(BF16), a2_ref[...], preferred_element_type=F32)
    for dr in range(N_DIR):
        y = -(w0_ref[dr:dr + 1, :] + lw[:, dr * d:(dr + 1) * d])
        softplus = jnp.maximum(y, 0.0) + jnp.log(1.0 + jnp.exp(-jnp.abs(y)))
        w_log = -softplus - 0.5
        w_ref[dr, 0] = jnp.exp(-jnp.exp(w_log))
        a_ref[dr, 0] = _sigmoid(a0_ref[dr:dr + 1, :] + la[:, dr * d:(dr + 1) * d])


def _prep(z3, mu_p, w2blk, a2blk, g2p, w0, a0, tt, d_rwkv):
    b, t, _ = z3.shape
    nt = t // tt
    tb = tt // 8
    nat = jax.ShapeDtypeStruct((b, t, d_rwkv), F32)
    nat2 = jax.ShapeDtypeStruct((N_DIR, b, t, d_rwkv), F32)
    full = lambda a: pl.BlockSpec(a.shape, lambda bi, i: (0,) * a.ndim)
    o1 = pl.BlockSpec((1, tt, d_rwkv), lambda bi, i: (bi, i, 0))
    o2 = pl.BlockSpec((N_DIR, 1, tt, d_rwkv), lambda bi, i: (0, bi, i, 0))
    return pl.pallas_call(
        functools.partial(_prep_kernel, d_rwkv=d_rwkv),
        grid=(b, nt),
        in_specs=[pl.BlockSpec((1, tt, ZR_PAD), lambda bi, i: (bi, i, 0)),
                  pl.BlockSpec((1, 8, ZR_PAD), lambda bi, i: (bi, jnp.maximum(i * tb - 1, 0), 0)),
                  pl.BlockSpec((1, 8, ZR_PAD), lambda bi, i: (bi, jnp.minimum((i + 1) * tb, t // 8 - 1), 0)),
                  full(mu_p), full(w2blk), full(a2blk), full(g2p), full(w0), full(a0)],
        out_specs=[o1, o1, o1, o2, o2, o1],
        out_shape=[nat, nat, nat, nat2, nat2, nat],
        compiler_params=_cparams(("parallel", "arbitrary")),
        name="prep",
    )(z3, z3, z3, mu_p, w2blk, a2blk, g2p, w0, a0)


def _scan_kernel(rc_ref, kc_ref, vc_ref, ac_ref, wc_ref, rl_ref, kl_ref, vl_ref, al_ref, wl_ref,
                 kk_ref, ka_ref, y_ref, s_ref, ops_ref, cur_ref, *, tt, n, nb_ctx):
    d = pl.program_id(0)
    i = pl.program_id(1)
    nvb = n // SUBLANES

    @pl.when(i == 0)
    def _():
        s_ref[...] = jnp.zeros_like(s_ref)

    k_k = kk_ref[...]
    k_a = ka_ref[...]

    def stage_operands(r_ref, k_ref, v_ref, a_ref, w_ref):
        k = k_ref[...]
        a = a_ref[0]
        kk = k * k_k
        nrm = jnp.sqrt(jnp.sum(kk * kk, axis=1, keepdims=True))
        kk = kk / jnp.maximum(nrm, 1e-12)
        ops_ref[0] = kk
        ops_ref[1] = w_ref[0]
        ops_ref[2] = -(kk * a)
        ops_ref[3] = k * (1.0 + (a - 1.0) * k_a)
        ops_ref[4] = r_ref[...]
        ops_ref[5] = v_ref[...]

    @pl.when(i < nb_ctx)
    def _():
        stage_operands(rc_ref, kc_ref, vc_ref, ac_ref, wc_ref)

    @pl.when(i >= nb_ctx)
    def _():
        stage_operands(rl_ref, kl_ref, vl_ref, al_ref, wl_ref)

    def step(j, carry):
        t = jnp.where(d == 0, j, tt - 1 - j)
        cur_ref[...] = ops_ref[:, t]
        row = lambda o, kx: cur_ref[o, pl.ds(kx, 1), :]
        tile = lambda kx, vb: s_ref[kx, pl.ds(vb * SUBLANES, SUBLANES), :]
        acc = [[None, None] for _ in range(nvb)]
        for kx in range(n):
            kb = row(0, kx)
            for vb in range(nvb):
                p = tile(kx, vb) * kb
                cur = acc[vb][kx % 2]
                acc[vb][kx % 2] = p if cur is None else cur + p
        skk = [acc[vb][0] + acc[vb][1] for vb in range(nvb)]
        vv = [cur_ref[5, pl.ds(vb * SUBLANES, SUBLANES), :] for vb in range(nvb)]
        yacc = [None] * nvb
        for kx in range(n):
            wb, nb_, kdb, rb = row(1, kx), row(2, kx), row(3, kx), row(4, kx)
            for vb in range(nvb):
                sn = tile(kx, vb) * wb + skk[vb] * nb_ + vv[vb] * kdb
                s_ref[kx, pl.ds(vb * SUBLANES, SUBLANES), :] = sn
                q = sn * rb
                yacc[vb] = q if yacc[vb] is None else yacc[vb] + q
        for vb in range(nvb):
            y_ref[0, t, pl.ds(vb * SUBLANES, SUBLANES), :] = yacc[vb]
        return carry

    lax.fori_loop(0, tt, step, 0)


def _scan(ctx_ops, lat_ops, kk_t, ka_t, tt):
    t_ctx, n, c = ctx_ops[0].shape
    t_lat = lat_ops[0].shape[0]
    nb_ctx, nb_lat = t_ctx // tt, t_lat // tt

    def blk_ctx(d, i):
        j = jnp.minimum(i, nb_ctx - 1)
        return jnp.where(d == 0, j, nb_ctx - 1 - j)

    def blk_lat(d, i):
        j = jnp.maximum(i - nb_ctx, 0)
        return jnp.where(d == 0, j, nb_lat - 1 - j)

    def specs(blk):
        s1 = pl.BlockSpec((tt, n, c), lambda d, i: (blk(d, i), 0, 0))
        s2 = pl.BlockSpec((1, tt, n, c), lambda d, i: (d, blk(d, i), 0, 0))
        return [s1, s1, s1, s2, s2], s2

    ctx_specs, _ = specs(blk_ctx)
    lat_specs, y_spec = specs(blk_lat)
    const = pl.BlockSpec((n, c), lambda d, i: (0, 0))
    return pl.pallas_call(
        functools.partial(_scan_kernel, tt=tt, n=n, nb_ctx=nb_ctx),
        grid=(N_DIR, nb_ctx + nb_lat),
        in_specs=ctx_specs + lat_specs + [const, const],
        out_specs=y_spec,
        out_shape=jax.ShapeDtypeStruct((N_DIR, t_lat, n, c), F32),
        scratch_shapes=[pltpu.VMEM((n, n, c), F32), pltpu.VMEM((6, tt, n, c), F32), pltpu.VMEM((6, n, c), F32)],
        compiler_params=_cparams(("arbitrary", "arbitrary")),
        name="scan",
    )(*ctx_ops, *lat_ops, kk_t, ka_t)


def _post_kernel(y_ref, r_ref, k_ref, v_ref, a_ref, ka_ref, rk_ref, g_ref, b_ref, o_ref):
    y = y_ref[0] + y_ref[1]
    mu = jnp.mean(y, axis=1, keepdims=True)
    yc = y - mu
    var = jnp.mean(yc * yc, axis=1, keepdims=True)
    yn = yc * lax.rsqrt(var + GN_EPS) * g_ref[...] + b_ref[...]
    r = r_ref[...]
    k = k_ref[...]
    k_a = ka_ref[...]
    rk = 0.0
    for dr in range(N_DIR):
        kd = k * (1.0 + (a_ref[dr] - 1.0) * k_a)
        rk = rk + jnp.sum(r * kd * rk_ref[...], axis=1, keepdims=True)
    o_ref[...] = yn + rk * v_ref[...]


def _post(y, r, k, v, a, ka_t, rk_t, g_t, b_t, tt, t_off):
    _, t_tot, n, c = y.shape
    ob = t_off // tt
    nb = (t_tot - t_off) // tt
    s1 = pl.BlockSpec((tt, n, c), lambda i: (i + ob, 0, 0))
    s2 = pl.BlockSpec((N_DIR, tt, n, c), lambda i: (0, i + ob, 0, 0))
    const = pl.BlockSpec((n, c), lambda i: (0, 0))
    return pl.pallas_call(
        _post_kernel,
        grid=(nb,),
        in_specs=[s2, s1, s1, s1, s2, const, const, const, const],
        out_specs=pl.BlockSpec((tt, n, c), lambda i: (i, 0, 0)),
        out_shape=jax.ShapeDtypeStruct((t_tot - t_off, n, c), F32),
        compiler_params=_cparams(("parallel",)),
        name="post",
    )(y, r, k, v, a, ka_t, rk_t, g_t, b_t)


def _conv_kernel(za_ref, zg_ref, cw_ref, cb_ref, lg_ref, lb_ref, wo_ref, bo_ref, o_ref):
    h = za_ref[...] * _sigmoid(zg_ref[...])
    tm = h.shape[0]
    pos = lax.broadcasted_iota(jnp.int32, (tm, 1), 0) % GRID_W
    half = CONV_WIDTH // 2
    acc = jnp.zeros_like(h)
    for j in range(CONV_WIDTH):
        off = j - half
        sh = pltpu.roll(h, (-off) % tm, 0) if off != 0 else h
        ok = jnp.logical_and(pos + off >= 0, pos + off < GRID_W)
        acc = acc + jnp.where(ok, sh, 0.0) * cw_ref[j:j + 1, :]
    acc = acc + cb_ref[...]
    hn = _ln(acc) * lg_ref[...] + lb_ref[...]
    act = hn * _sigmoid(hn)
    o_ref[...] = jnp.dot(act.astype(BF16), wo_ref[...], preferred_element_type=F32) + bo_ref[...]


def _conv(z, col_a, col_g, cw, cb, lg, lb, wo_bf16, bo, tm):
    n_rows = z.shape[0]
    dc = cw.shape[1]
    dm = wo_bf16.shape[1]
    full = lambda a: pl.BlockSpec(a.shape, lambda m: (0,) * a.ndim)
    cb, lg, lb, bo = cb.reshape(1, dc), lg.reshape(1, dc), lb.reshape(1, dc), bo.reshape(1, dm)
    return pl.pallas_call(
        _conv_kernel,
        grid=(n_rows // tm,),
        in_specs=[pl.BlockSpec((tm, dc), lambda m: (m, col_a)),
                  pl.BlockSpec((tm, dc), lambda m: (m, col_g)),
                  full(cw), full(cb), full(lg), full(lb), full(wo_bf16), full(bo)],
        out_specs=pl.BlockSpec((tm, dm), lambda m: (m, 0)),
        out_shape=jax.ShapeDtypeStruct((n_rows, dm), F32),
        compiler_params=_cparams(("parallel",)),
        name="conv",
    )(z, z, cw, cb, lg, lb, wo_bf16, bo)


def _merge_kernel(o_ref, g_ref, yc_ref, z1_ref, z2_ref, x_ref, gate_ref, wr_ref, wo_ref, bo_ref,
                  lg_ref, lb_ref, out_ref):
    y_rwkv = jnp.dot((o_ref[...] * g_ref[...]).astype(BF16), wr_ref[...], preferred_element_type=F32)
    mix = _sigmoid(z1_ref[...]) * yc_ref[...] + _sigmoid(z2_ref[...]) * y_rwkv
    out = jnp.dot(mix.astype(BF16), wo_ref[...], preferred_element_type=F32) + bo_ref[...]
    h = DEEPNORM_ALPHA * x_ref[...] + gate_ref[0] * out
    out_ref[...] = _ln(h) * lg_ref[...] + lb_ref[...]


def _merge(o_nat, g_nat, y_conv, z, col_g1, x2, mod3, wr_bf16, wo_bf16, bo, lg, lb, tm, rows_per_batch):
    n_rows, dm = x2.shape
    dr = o_nat.shape[1]
    full = lambda a: pl.BlockSpec(a.shape, lambda m: (0,) * a.ndim)
    bo, lg, lb = bo.reshape(1, dm), lg.reshape(1, dm), lb.reshape(1, dm)
    tiles_per_batch = rows_per_batch // tm
    return pl.pallas_call(
        _merge_kernel,
        grid=(n_rows // tm,),
        in_specs=[pl.BlockSpec((tm, dr), lambda m: (m, 0)),
                  pl.BlockSpec((tm, dr), lambda m: (m, 0)),
                  pl.BlockSpec((tm, dm), lambda m: (m, 0)),
                  pl.BlockSpec((tm, dm), lambda m: (m, col_g1)),
                  pl.BlockSpec((tm, dm), lambda m: (m, col_g1 + 1)),
                  pl.BlockSpec((tm, dm), lambda m: (m, 0)),
                  pl.BlockSpec((1, 1, dm), lambda m: ((m // tiles_per_batch) * 6 + 2, 0, 0)),
                  full(wr_bf16), full(wo_bf16), full(bo), full(lg), full(lb)],
        out_specs=pl.BlockSpec((tm, dm), lambda m: (m, 0)),
        out_shape=jax.ShapeDtypeStruct((n_rows, dm), F32),
        compiler_params=_cparams(("parallel",)),
        name="merge",
    )(o_nat, g_nat, y_conv, z, z, x2, mod3, wr_bf16, wo_bf16, bo, lg, lb)


def _router_kernel(x_ref, sh_ref, sc_ref, wr_ref, br_ref, u_ref, idx_ref, gate_ref):
    u = _ln(x_ref[...]) * (1.0 + sc_ref[0]) + sh_ref[0]
    u_ref[...] = u
    logits = jnp.dot(u, wr_ref[...], preferred_element_type=F32, precision=HIGHEST) + br_ref[...]
    lane = lax.broadcasted_iota(jnp.int32, logits.shape, 1)
    vals, idxs = [], []
    l = logits
    for _ in range(TOP_K):
        m = jnp.max(l, axis=-1, keepdims=True)
        ix = jnp.min(jnp.where(l == m, lane, LANES), axis=-1, keepdims=True)
        vals.append(m)
        idxs.append(ix)
        l = jnp.where(lane == ix, -jnp.inf, l)
    es = [jnp.exp(vv - vals[0]) for vv in vals]
    den = es[0] + es[1] + es[2] + es[3]
    idx_out = jnp.zeros(logits.shape, jnp.int32)
    gate_out = jnp.zeros(logits.shape, F32)
    for j in range(TOP_K):
        idx_out = jnp.where(lane == j, idxs[j], idx_out)
        gate_out = jnp.where(lane == j, es[j] / den, gate_out)
    idx_ref[...] = idx_out
    gate_ref[...] = gate_out


def _router(x_mid, mod3, w_router_p, b_router_p, tm, rows_per_batch):
    n_rows, dm = x_mid.shape
    tiles_per_batch = rows_per_batch // tm
    full = lambda a: pl.BlockSpec(a.shape, lambda m: (0,) * a.ndim)
    row = pl.BlockSpec((tm, dm), lambda m: (m, 0))
    lane_out = pl.BlockSpec((tm, LANES), lambda m: (m, 0))
    return pl.pallas_call(
        _router_kernel,
        grid=(n_rows // tm,),
        in_specs=[row,
                  pl.BlockSpec((1, 1, dm), lambda m: ((m // tiles_per_batch) * 6 + 3, 0, 0)),
                  pl.BlockSpec((1, 1, dm), lambda m: ((m // tiles_per_batch) * 6 + 4, 0, 0)),
                  full(w_router_p), full(b_router_p)],
        out_specs=[row, lane_out, lane_out],
        out_shape=[jax.ShapeDtypeStruct((n_rows, dm), F32),
                   jax.ShapeDtypeStruct((n_rows, LANES), jnp.int32),
                   jax.ShapeDtypeStruct((n_rows, LANES), F32)],
        compiler_params=_cparams(("parallel",)),
        name="router",
    )(x_mid, mod3, mod3, w_router_p, b_router_p)


def _group_changed(te_ref, m):
    return jnp.logical_or(m == 0, te_ref[m] != te_ref[jnp.maximum(m - 1, 0)])


def _gmm1_kernel(te_ref, na_ref, x_ref, wg_ref, wu_ref, bg_ref, bu_ref, o_ref, wgb_ref, wub_ref):
    m = pl.program_id(1)

    @pl.when(_group_changed(te_ref, m))
    def _():
        wgb_ref[...] = wg_ref[0].astype(BF16)
        wub_ref[...] = wu_ref[0].astype(BF16)

    @pl.when(m < na_ref[0])
    def _():
        x = x_ref[...]
        gate = jnp.dot(x, wgb_ref[...], preferred_element_type=F32) + bg_ref[0]
        up = jnp.dot(x, wub_ref[...], preferred_element_type=F32) + bu_ref[0]
        gate = jnp.minimum(gate, SWIGLU_LIMIT)
        up = jnp.clip(up, -SWIGLU_LIMIT, SWIGLU_LIMIT)
        hid = (up + 1.0) * (gate * _sigmoid(SWIGLU_ALPHA * gate))
        o_ref[...] = hid.astype(BF16)

    @pl.when(m >= na_ref[0])
    def _():
        o_ref[...] = jnp.zeros_like(o_ref)


def _active_tile(m, na):
    return jnp.minimum(m, na[0] - 1)


def _gmm1(tile_expert, n_active, xs, w_gate_up, b_gate_up3, tm, th):
    p_pad, d = xs.shape
    de = w_gate_up.shape[2] // 2
    nj = de // th
    grid_spec = pltpu.PrefetchScalarGridSpec(
        num_scalar_prefetch=2,
        grid=(nj, p_pad // tm),
        in_specs=[pl.BlockSpec((tm, d), lambda j, m, te, na: (_active_tile(m, na), 0)),
                  pl.BlockSpec((1, d, th), lambda j, m, te, na: (te[m], 0, j)),
                  pl.BlockSpec((1, d, th), lambda j, m, te, na: (te[m], 0, j + nj)),
                  pl.BlockSpec((1, 1, th), lambda j, m, te, na: (te[m], 0, j)),
                  pl.BlockSpec((1, 1, th), lambda j, m, te, na: (te[m], 0, j + nj))],
        out_specs=pl.BlockSpec((tm, th), lambda j, m, te, na: (m, j)),
        scratch_shapes=[pltpu.VMEM((d, th), BF16), pltpu.VMEM((d, th), BF16)],
    )
    return pl.pallas_call(
        _gmm1_kernel,
        grid_spec=grid_spec,
        out_shape=jax.ShapeDtypeStruct((p_pad, de), BF16),
        compiler_params=_cparams(("arbitrary", "arbitrary")),
        name="gmm1",
    )(tile_expert, n_active, xs, w_gate_up, w_gate_up, b_gate_up3, b_gate_up3)


def _gmm2_kernel(te_ref, na_ref, h_ref, w_ref, b_ref, o_ref, wb_ref):
    m = pl.program_id(1)

    @pl.when(_group_changed(te_ref, m))
    def _():
        wb_ref[...] = w_ref[0].astype(BF16)

    @pl.when(m < na_ref[0])
    def _():
        o_ref[...] = jnp.dot(h_ref[...], wb_ref[...], preferred_element_type=F32) + b_ref[0]

    @pl.when(m >= na_ref[0])
    def _():
        o_ref[...] = jnp.zeros_like(o_ref)


def _gmm2(tile_expert, n_active, hid, w_down, b_down3, tm, tn):
    p_pad, de = hid.shape
    d = w_down.shape[2]
    grid_spec = pltpu.PrefetchScalarGridSpec(
        num_scalar_prefetch=2,
        grid=(d // tn, p_pad // tm),
        in_specs=[pl.BlockSpec((tm, de), lambda j, m, te, na: (_active_tile(m, na), 0)),
                  pl.BlockSpec((1, de, tn), lambda j, m, te, na: (te[m], 0, j)),
                  pl.BlockSpec((1, 1, tn), lambda j, m, te, na: (te[m], 0, j))],
        out_specs=pl.BlockSpec((tm, tn), lambda j, m, te, na: (m, j)),
        scratch_shapes=[pltpu.VMEM((de, tn), BF16)],
    )
    return pl.pallas_call(
        _gmm2_kernel,
        grid_spec=grid_spec,
        out_shape=jax.ShapeDtypeStruct((p_pad, d), F32),
        compiler_params=_cparams(("arbitrary", "arbitrary")),
        name="gmm2",
    )(tile_expert, n_active, hid, w_down, b_down3)


def _row_copy(src_hbm, src_row, buf_ref, slot, dst_row, sem_ref):
    return pltpu.make_async_copy(src_hbm.at[pl.ds(src_row, 1), :],
                                 buf_ref.at[slot, pl.ds(dst_row, 1), :], sem_ref.at[slot])


def _wait_rows(src_hbm, buf_ref, slot, sem_ref):
    n = buf_ref.shape[1]
    pltpu.make_async_copy(src_hbm.at[pl.ds(0, n), :], buf_ref.at[slot], sem_ref.at[slot]).wait()


def _dispatch_kernel(rt_ref, na_ref, u_hbm, o_ref, buf_ref, sem_ref, *, tm):
    i = pl.program_id(0)
    na = na_ref[0]

    def issue(tile, slot):
        def body(r, carry):
            _row_copy(u_hbm, rt_ref[tile * tm + r], buf_ref, slot, r, sem_ref).start()
            return carry
        lax.fori_loop(0, tm, body, 0, unroll=ISSUE_UNROLL)

    @pl.when(jnp.logical_and(i == 0, na > 0))
    def _():
        issue(0, 0)

    @pl.when(i + 1 < na)
    def _():
        issue(i + 1, (i + 1) % 2)

    @pl.when(i < na)
    def _():
        slot = i % 2
        _wait_rows(u_hbm, buf_ref, slot, sem_ref)
        o_ref[...] = buf_ref[slot].astype(BF16)

    @pl.when(i >= na)
    def _():
        o_ref[...] = jnp.zeros_like(o_ref)


def _dispatch(row_token, n_active, u_f, tm):
    p_pad = row_token.shape[0]
    dm = u_f.shape[1]
    grid_spec = pltpu.PrefetchScalarGridSpec(
        num_scalar_prefetch=2,
        grid=(p_pad // tm,),
        in_specs=[pl.BlockSpec(memory_space=pl.ANY)],
        out_specs=pl.BlockSpec((tm, dm), lambda i, rt, na: (i, 0)),
        scratch_shapes=[pltpu.VMEM((2, tm, dm), F32), pltpu.SemaphoreType.DMA((2,))],
    )
    return pl.pallas_call(
        functools.partial(_dispatch_kernel, tm=tm),
        grid_spec=grid_spec,
        out_shape=jax.ShapeDtypeStruct((p_pad, dm), BF16),
        compiler_params=_cparams(("arbitrary",)),
        name="dispatch",
    )(row_token, n_active, u_f)


def _combine_kernel(pos_ref, ys_hbm, x_ref, gl_ref, gate_ref, lg_ref, lb_ref, o_ref, buf_ref, sem_ref, *, tm):
    i = pl.program_id(0)
    n = pl.num_programs(0)

    def issue(tile, slot):
        def body(r, carry):
            for j in range(TOP_K):
                p = pos_ref[(tile * tm + r) * TOP_K + j]
                _row_copy(ys_hbm, p, buf_ref, slot, j * tm + r, sem_ref).start()
            return carry
        lax.fori_loop(0, tm, body, 0, unroll=ISSUE_UNROLL // TOP_K)

    @pl.when(i == 0)
    def _():
        issue(0, 0)

    @pl.when(i + 1 < n)
    def _():
        issue(i + 1, (i + 1) % 2)

    slot = i % 2
    _wait_rows(ys_hbm, buf_ref, slot, sem_ref)
    gl = gl_ref[...]
    moe = gl[:, 0:1] * buf_ref[slot, pl.ds(0, tm), :]
    for j in range(1, TOP_K):
        moe = moe + gl[:, j:j + 1] * buf_ref[slot, pl.ds(j * tm, tm), :]
    h = DEEPNORM_ALPHA * x_ref[...] + gate_ref[0] * moe
    o_ref[...] = _ln(h) * lg_ref[...] + lb_ref[...]


def _combine(pos, ys, x_mid, gate_l, mod3, lg, lb, tm, rows_per_batch):
    n_rows, dm = x_mid.shape
    tiles_per_batch = rows_per_batch // tm
    row = pl.BlockSpec((tm, dm), lambda m, p: (m, 0))
    vec = pl.BlockSpec((1, dm), lambda m, p: (0, 0))
    grid_spec = pltpu.PrefetchScalarGridSpec(
        num_scalar_prefetch=1,
        grid=(n_rows // tm,),
        in_specs=[pl.BlockSpec(memory_space=pl.ANY), row,
                  pl.BlockSpec((tm, LANES), lambda m, p: (m, 0)),
                  pl.BlockSpec((1, 1, dm), lambda m, p: ((m // tiles_per_batch) * 6 + 5, 0, 0)),
                  vec, vec],
        out_specs=row,
        scratch_shapes=[pltpu.VMEM((2, TOP_K * tm, dm), F32), pltpu.SemaphoreType.DMA((2,))],
    )
    return pl.pallas_call(
        functools.partial(_combine_kernel, tm=tm),
        grid_spec=grid_spec,
        out_shape=jax.ShapeDtypeStruct((n_rows, dm), F32),
        compiler_params=_cparams(("arbitrary",)),
        name="combine",
    )(pos, ys, x_mid, gate_l, mod3, lg.reshape(1, dm), lb.reshape(1, dm))


def _to_scan_layout(a, n_heads):
    *lead, b, t, _ = a.shape
    nl = len(lead)
    a = a.reshape(*lead, b, t, n_heads, HEAD_SIZE)
    perm = tuple(range(nl)) + (nl + 1, nl + 3, nl + 0, nl + 2)
    return a.transpose(perm).reshape(*lead, t, HEAD_SIZE, b * n_heads)


def _head_tile(p, n_heads, batch):
    return jnp.tile(p.reshape(n_heads, HEAD_SIZE).T, (1, batch))


def _pick(n, pref):
    t = min(n, pref)
    while n % t:
        t //= 2
    return t


def kernel(x, c, ctx, c_ctx, w_ada, b_ada, w_in, b_in, shift_mu, conv_w, conv_b, conv_ln_g, conv_ln_b,
           w_conv_o, b_conv_o, w0, w2, a0, a2, g2, k_k, k_a, r_k, lnx_g, lnx_b, w_rwkv_o, w_out, b_out,
           ln1_g, ln1_b, w_router, b_router, w_gate_up, b_gate_up, w_down, b_down, ln2_g, ln2_b):
    bsz, seq, dm = x.shape
    n_ctx = ctx.shape[1]
    dc = conv_w.shape[2]
    dr = k_k.shape[1]
    n_heads = dr // HEAD_SIZE
    d_shift = shift_mu.shape[2]
    n_exp = w_router.shape[2]
    l = 0
    n_tok = bsz * seq

    c_rows = jnp.zeros((16, dm), F32).at[:bsz].set(c).at[bsz].set(c_ctx)
    mod = _ada(c_rows, w_ada[l], b_ada[l])
    mod3 = mod.reshape(16 * 6, 1, dm)

    pad = ZR_PAD - d_shift
    wl = w_in[l]
    w_p = jnp.concatenate([wl[:, 2 * dc:2 * dc + d_shift], jnp.zeros((dm, pad), F32),
                           wl[:, :2 * dc], wl[:, 2 * dc + d_shift:]], axis=1).astype(BF16)
    bl = b_in[l]
    b_p = jnp.concatenate([bl[2 * dc:2 * dc + d_shift], jnp.zeros((pad,), F32),
                           bl[:2 * dc], bl[2 * dc + d_shift:]])
    x2 = x.reshape(n_tok, dm)
    tm_in = _pick(seq, 1024)
    z_l = _lnmod_mm(x2, mod3, lambda m: (m * tm_in) // seq, 0, w_p, b_p, tm_in, 1024)
    tm_c = _pick(bsz * n_ctx, 1024)
    z_c = _lnmod_mm(ctx.reshape(bsz * n_ctx, dm), mod3, lambda m: bsz, 0,
                    w_p[:, :ZR_PAD], b_p[:ZR_PAD], tm_c, 1024)

    mu_p = jnp.pad(shift_mu[l], ((0, 0), (0, pad)))
    w2blk = jnp.zeros((LANES, N_DIR * dr), F32)
    a2blk = jnp.zeros((LANES, N_DIR * dr), F32)
    for d in range(N_DIR):
        w2blk = w2blk.at[d * D_LORA:(d + 1) * D_LORA, d * dr:(d + 1) * dr].set(w2[l, d])
        a2blk = a2blk.at[d * D_LORA:(d + 1) * D_LORA, d * dr:(d + 1) * dr].set(a2[l, d])
    g2p = jnp.zeros((2 * LANES, dr), F32).at[:D_GATE_LORA].set(g2[l])
    prep_args = (mu_p, w2blk.astype(BF16), a2blk.astype(BF16), g2p.astype(BF16), w0[l], a0[l])
    tt_p = _pick(seq, 256)
    r_l, k_l, v_l, a_l, w_l, g_l = _prep(z_l.reshape(bsz, seq, -1), *prep_args, tt_p, dr)
    r_c, k_c, v_c, a_c, w_c, _ = _prep(z_c.reshape(bsz, n_ctx, -1), *prep_args, _pick(n_ctx, 256), dr)

    ctx_ops = tuple(_to_scan_layout(t, n_heads) for t in (r_c, k_c, v_c, a_c, w_c))
    lat_ops = tuple(_to_scan_layout(t, n_heads) for t in (r_l, k_l, v_l, a_l, w_l))
    ka_t = _head_tile(k_a[l], n_heads, bsz)
    tt_s = _pick(n_ctx, 32)
    y_s = _scan(ctx_ops, lat_ops, _head_tile(k_k[l], n_heads, bsz), ka_t, tt_s)
    o_s = _post(y_s, *lat_ops[:4], ka_t, jnp.tile(r_k[l].T, (1, bsz)),
                _head_tile(lnx_g[l], n_heads, bsz), _head_tile(lnx_b[l], n_heads, bsz), tt_s, 0)
    o_nat = o_s.reshape(seq, HEAD_SIZE, bsz, n_heads).transpose(2, 0, 3, 1).reshape(n_tok, dr)

    zcol = ZR_PAD // dc
    y_conv = _conv(z_l, zcol, zcol + 1, conv_w[l], conv_b[l], conv_ln_g[l], conv_ln_b[l],
                   w_conv_o[l].astype(BF16), b_conv_o[l], _pick(seq, 512))
    gcol = (ZR_PAD + 2 * dc) // dm
    tm_m = _pick(seq, 128)
    x_mid = _merge(o_nat, g_l.reshape(n_tok, dr), y_conv, z_l, gcol, x2, mod3, w_rwkv_o[l].astype(BF16),
                   w_out[l].astype(BF16), b_out[l], ln1_g[l], ln1_b[l], tm_m, seq)

    w_r_p = jnp.zeros((dm, LANES), F32).at[:, :n_exp].set(w_router[l])
    b_r_p = jnp.full((1, LANES), -1e30, F32).at[0, :n_exp].set(b_router[l])
    u_f, idx_l, gate_l = _router(x_mid, mod3, w_r_p, b_r_p, _pick(seq, 256), seq)
    top_idx = idx_l[:, :TOP_K]

    tm_e = EXPERT_ROW_TILE
    n_pair = n_tok * TOP_K
    n_tiles = -(-n_pair // tm_e) + n_exp
    p_pad = n_tiles * tm_e
    eq = top_idx[:, :, None] == jnp.arange(n_exp, dtype=jnp.int32)[None, None, :]
    onehot = jnp.sum(eq, axis=1, dtype=jnp.int32)
    cnt_incl = jnp.cumsum(onehot, axis=0)
    counts = cnt_incl[-1]
    cnt_excl = cnt_incl - onehot
    padded = ((counts + tm_e - 1) // tm_e) * tm_e
    pend = jnp.cumsum(padded)
    pstart = pend - padded
    pos = jnp.sum(jnp.where(eq, (cnt_excl + pstart[None, :])[:, None, :], 0), axis=2).reshape(-1)
    row_token = jnp.zeros((p_pad,), jnp.int32).at[pos].set(
        jnp.arange(n_pair, dtype=jnp.int32) // TOP_K, unique_indices=True, mode="promise_in_bounds")
    n_active = (pend[-1] // tm_e).astype(jnp.int32).reshape(1)
    tile_start = jnp.arange(n_tiles, dtype=jnp.int32) * tm_e
    tile_expert = jnp.sum(pend[None, :] <= tile_start[:, None], axis=1, dtype=jnp.int32)
    last_e = jnp.sum(pend <= pend[-1] - 1, dtype=jnp.int32)
    tile_expert = jnp.minimum(tile_expert, last_e)

    xs = _dispatch(row_token, n_active, u_f, tm_e)
    hid = _gmm1(tile_expert, n_active, xs, w_gate_up[l], b_gate_up[l].reshape(n_exp, 1, -1), tm_e, 1024)
    ys = _gmm2(tile_expert, n_active, hid, w_down[l], b_down[l].reshape(n_exp, 1, -1), tm_e, 1024)
    out = _combine(pos, ys, x_mid, gate_l, mod3, ln2_g[l], ln2_b[l], _pick(seq, 128), seq)
    return out.reshape(bsz, seq, dm)
```

```python
import functools

import jax
import jax.numpy as jnp
from jax import lax
from jax.experimental import pallas as pl
from jax.experimental.pallas import tpu as pltpu

F32 = jnp.float32
BF16 = jnp.bfloat16
HIGHEST = lax.Precision.HIGHEST

GRID_W = 64
CONV_WIDTH = 31
HEAD_SIZE = 64
D_LORA = 64
D_GATE_LORA = 160
N_DIR = 2
GN_EPS = 64e-5
LN_EPS = 1e-5
TOP_K = 4
SWIGLU_LIMIT = 7.0
SWIGLU_ALPHA = 1.702
DEPTH = 1
DEEPNORM_ALPHA = (2.0 * DEPTH) ** 0.25

LANES = 128
SUBLANES = 8
ZR_PAD = 4096
VMEM_LIMIT = 56 * 1024 * 1024
EXPERT_ROW_TILE = 512
ISSUE_UNROLL = 8


def _cparams(sem):
    return pltpu.CompilerParams(dimension_semantics=sem, vmem_limit_bytes=VMEM_LIMIT)


def _ln(x):
    mu = jnp.mean(x, axis=-1, keepdims=True)
    xc = x - mu
    var = jnp.mean(xc * xc, axis=-1, keepdims=True)
    return xc * lax.rsqrt(var + LN_EPS)


def _sigmoid(x):
    return 1.0 / (1.0 + jnp.exp(-x))


def _ada_kernel(c_ref, w_ref, b_ref, o_ref):
    c = c_ref[...]
    s = c * _sigmoid(c)
    o_ref[...] = jnp.dot(s, w_ref[...], preferred_element_type=F32, precision=HIGHEST) + b_ref[...]


def _ada(c_rows, w, b):
    rows, d = c_rows.shape
    n = w.shape[1]
    tn = 1024
    return pl.pallas_call(
        _ada_kernel,
        grid=(n // tn,),
        in_specs=[pl.BlockSpec((rows, d), lambda j: (0, 0)),
                  pl.BlockSpec((d, tn), lambda j: (0, j)),
                  pl.BlockSpec((1, tn), lambda j: (0, j))],
        out_specs=pl.BlockSpec((rows, tn), lambda j: (0, j)),
        out_shape=jax.ShapeDtypeStruct((rows, n), F32),
        compiler_params=_cparams(("arbitrary",)),
        name="ada",
    )(c_rows, w, b.reshape(1, n))


def _lnmod_mm_kernel(x_ref, sh_ref, sc_ref, w_ref, b_ref, o_ref, xn_ref):
    @pl.when(pl.program_id(1) == 0)
    def _():
        xn = _ln(x_ref[...]) * (1.0 + sc_ref[0]) + sh_ref[0]
        xn_ref[...] = xn.astype(BF16)

    o_ref[...] = jnp.dot(xn_ref[...], w_ref[...], preferred_element_type=F32) + b_ref[...]


def _lnmod_mm(x2, mod3, row_of_tile, shift_slot, w_bf16, bias, tm, tn):
    m_rows, d = x2.shape
    n = w_bf16.shape[1]
    return pl.pallas_call(
        _lnmod_mm_kernel,
        grid=(m_rows // tm, n // tn),
        in_specs=[pl.BlockSpec((tm, d), lambda m, j: (m, 0)),
                  pl.BlockSpec((1, 1, d), lambda m, j: (row_of_tile(m) * 6 + shift_slot, 0, 0)),
                  pl.BlockSpec((1, 1, d), lambda m, j: (row_of_tile(m) * 6 + shift_slot + 1, 0, 0)),
                  pl.BlockSpec((d, tn), lambda m, j: (0, j)),
                  pl.BlockSpec((1, tn), lambda m, j: (0, j))],
        out_specs=pl.BlockSpec((tm, tn), lambda m, j: (m, j)),
        out_shape=jax.ShapeDtypeStruct((m_rows, n), F32),
        scratch_shapes=[pltpu.VMEM((tm, d), BF16)],
        compiler_params=_cparams(("parallel", "arbitrary")),
        name="lnmod_mm",
    )(x2, mod3, mod3, w_bf16, bias.reshape(1, n))


def _prep_kernel(z_ref, zp_ref, zn_ref, mu_ref, w2_ref, a2_ref, g2_ref, w0_ref, a0_ref,
                 r_ref, k_ref, v_ref, a_ref, w_ref, g_ref, *, d_rwkv):
    i = pl.program_id(1)
    nt = pl.num_programs(1)
    z = z_ref[0]
    tt = z.shape[0]
    rows = lax.broadcasted_iota(jnp.int32, (tt, 1), 0)
    halo_p = jnp.where(i == 0, 0.0, zp_ref[0, 7:8, :])
    halo_n = jnp.where(i == nt - 1, 0.0, zn_ref[0, 0:1, :])
    prev = jnp.where(rows == 0, halo_p, pltpu.roll(z, 1, 0))
    nxt = jnp.where(rows == tt - 1, halo_n, pltpu.roll(z, tt - 1, 0))
    zs = z + mu_ref[0:1, :] * (prev - z) + mu_ref[1:2, :] * (nxt - z)
    d = d_rwkv
    r_ref[0] = zs[:, 0:d]
    k_ref[0] = zs[:, d:2 * d]
    v_ref[0] = zs[:, 2 * d:3 * d]
    wd = zs[:, 3 * d:3 * d + LANES]
    ad = zs[:, 3 * d + LANES:3 * d + 2 * LANES]
    gd = zs[:, 3 * d + 2 * LANES:3 * d + 4 * LANES]
    g_ref[0] = jnp.dot(_sigmoid(gd).astype(BF16), g2_ref[...], preferred_element_type=F32)
    lw = jnp.dot(jnp.tanh(wd).astype(BF16), w2_ref[...], preferred_element_type=F32)
    la = jnp.dot(ad.astype(BF16), a2_ref[...], preferred_element_type=F32)
    for dr in range(N_DIR):
        y = -(w0_ref[dr:dr + 1, :] + lw[:, dr * d:(dr + 1) * d])
        softplus = jnp.maximum(y, 0.0) + jnp.log(1.0 + jnp.exp(-jnp.abs(y)))
        w_log = -softplus - 0.5
        w_ref[dr, 0] = jnp.exp(-jnp.exp(w_log))
        a_ref[dr, 0] = _sigmoid(a0_ref[dr:dr + 1, :] + la[:, dr * d:(dr + 1) * d])


def _prep(z3, mu_p, w2blk, a2blk, g2p, w0, a0, tt, d_rwkv):
    b, t, _ = z3.shape
    nt = t // tt
    tb = tt // 8
    nat = jax.ShapeDtypeStruct((b, t, d_rwkv), F32)
    nat2 = jax.ShapeDtypeStruct((N_DIR, b, t, d_rwkv), F32)
    full = lambda a: pl.BlockSpec(a.shape, lambda bi, i: (0,) * a.ndim)
    o1 = pl.BlockSpec((1, tt, d_rwkv), lambda bi, i: (bi, i, 0))
    o2 = pl.BlockSpec((N_DIR, 1, tt, d_rwkv), lambda bi, i: (0, bi, i, 0))
    return pl.pallas_call(
        functools.partial(_prep_kernel, d_rwkv=d_rwkv),
        grid=(b, nt),
        in_specs=[pl.BlockSpec((1, tt, ZR_PAD), lambda bi, i: (bi, i, 0)),
                  pl.BlockSpec((1, 8, ZR_PAD), lambda bi, i: (bi, jnp.maximum(i * tb - 1, 0), 0)),
                  pl.BlockSpec((1, 8, ZR_PAD), lambda bi, i: (bi, jnp.minimum((i + 1) * tb, t // 8 - 1), 0)),
                  full(mu_p), full(w2blk), full(a2blk), full(g2p), full(w0), full(a0)],
        out_specs=[o1, o1, o1, o2, o2, o1],
        out_shape=[nat, nat, nat, nat2, nat2, nat],
        compiler_params=_cparams(("parallel", "arbitrary")),
        name="prep",
    )(z3, z3, z3, mu_p, w2blk, a2blk, g2p, w0, a0)


def _scan_kernel(rc_ref, kc_ref, vc_ref, ac_ref, wc_ref, rl_ref, kl_ref, vl_ref, al_ref, wl_ref,
                 kk_ref, ka_ref, y_ref, s_ref, ops_ref, cur_ref, *, tt, n, nb_ctx):
    d = pl.program_id(0)
    i = pl.program_id(1)
    nvb = n // SUBLANES

    @pl.when(i == 0)
    def _():
        s_ref[...] = jnp.zeros_like(s_ref)

    k_k = kk_ref[...]
    k_a = ka_ref[...]

    def stage_operands(r_ref, k_ref, v_ref, a_ref, w_ref):
        k = k_ref[...]
        a = a_ref[0]
        kk = k * k_k
        nrm = jnp.sqrt(jnp.sum(kk * kk, axis=1, keepdims=True))
        kk = kk / jnp.maximum(nrm, 1e-12)
        ops_ref[0] = kk
        ops_ref[1] = w_ref[0]
        ops_ref[2] = -(kk * a)
        ops_ref[3] = k * (1.0 + (a - 1.0) * k_a)
        ops_ref[4] = r_ref[...]
        ops_ref[5] = v_ref[...]

    @pl.when(i < nb_ctx)
    def _():
        stage_operands(rc_ref, kc_ref, vc_ref, ac_ref, wc_ref)

    @pl.when(i >= nb_ctx)
    def _():
        stage_operands(rl_ref, kl_ref, vl_ref, al_ref, wl_ref)

    def step(j, carry):
        t = jnp.where(d == 0, j, tt - 1 - j)
        cur_ref[...] = ops_ref[:, t]
        row = lambda o, kx: cur_ref[o, pl.ds(kx, 1), :]
        tile = lambda kx, vb: s_ref[kx, pl.ds(vb * SUBLANES, SUBLANES), :]
        acc = [[None, None] for _ in range(nvb)]
        for kx in range(n):
            kb = row(0, kx)
            for vb in range(nvb):
                p = tile(kx, vb) * kb
                cur = acc[vb][kx % 2]
                acc[vb][kx % 2] = p if cur is None else cur + p
        skk = [acc[vb][0] + acc[vb][1] for vb in range(nvb)]
        vv = [cur_ref[5, pl.ds(vb * SUBLANES, SUBLANES), :] for vb in range(nvb)]
        yacc = [None] * nvb
        for kx in range(n):
            wb, nb_, kdb, rb = row(1, kx), row(2, kx), row(3, kx), row(4, kx)
            for vb in range(nvb):
                sn = tile(kx, vb) * wb + skk[vb] * nb_ + vv[vb] * kdb
                s_ref[kx, pl.ds(vb * SUBLANES, SUBLANES), :] = sn
                q = sn * rb
                yacc[vb] = q if yacc[vb] is None else yacc[vb] + q
        for vb in range(nvb):
            y_ref[0, t, pl.ds(vb * SUBLANES, SUBLANES), :] = yacc[vb]
        return carry

    lax.fori_loop(0, tt, step, 0)


def _scan(ctx_ops, lat_ops, kk_t, ka_t, tt):
    t_ctx, n, c = ctx_ops[0].shape
    t_lat = lat_ops[0].shape[0]
    nb_ctx, nb_lat = t_ctx // tt, t_lat // tt

    def blk_ctx(d, i):
        j = jnp.minimum(i, nb_ctx - 1)
        return jnp.where(d == 0, j, nb_ctx - 1 - j)

    def blk_lat(d, i):
        j = jnp.maximum(i - nb_ctx, 0)
        return jnp.where(d == 0, j, nb_lat - 1 - j)

    def specs(blk):
        s1 = pl.BlockSpec((tt, n, c), lambda d, i: (blk(d, i), 0, 0))
        s2 = pl.BlockSpec((1, tt, n, c), lambda d, i: (d, blk(d, i), 0, 0))
        return [s1, s1, s1, s2, s2], s2

    ctx_specs, _ = specs(blk_ctx)
    lat_specs, y_spec = specs(blk_lat)
    const = pl.BlockSpec((n, c), lambda d, i: (0, 0))
    return pl.pallas_call(
        functools.partial(_scan_kernel, tt=tt, n=n, nb_ctx=nb_ctx),
        grid=(N_DIR, nb_ctx + nb_lat),
        in_specs=ctx_specs + lat_specs + [const, const],
        out_specs=y_spec,
        out_shape=jax.ShapeDtypeStruct((N_DIR, t_lat, n, c), F32),
        scratch_shapes=[pltpu.VMEM((n, n, c), F32), pltpu.VMEM((6, tt, n, c), F32), pltpu.VMEM((6, n, c), F32)],
        compiler_params=_cparams(("arbitrary", "arbitrary")),
        name="scan",
    )(*ctx_ops, *lat_ops, kk_t, ka_t)


def _post_kernel(y_ref, r_ref, k_ref, v_ref, a_ref, ka_ref, rk_ref, g_ref, b_ref, o_ref):
    y = y_ref[0] + y_ref[1]
    mu = jnp.mean(y, axis=1, keepdims=True)
    yc = y - mu
    var = jnp.mean(yc * yc, axis=1, keepdims=True)
    yn = yc * lax.rsqrt(var + GN_EPS) * g_ref[...] + b_ref[...]
    r = r_ref[...]
    k = k_ref[...]
    k_a = ka_ref[...]
    rk = 0.0
    for dr in range(N_DIR):
        kd = k * (1.0 + (a_ref[dr] - 1.0) * k_a)
        rk = rk + jnp.sum(r * kd * rk_ref[...], axis=1, keepdims=True)
    o_ref[...] = yn + rk * v_ref[...]


def _post(y, r, k, v, a, ka_t, rk_t, g_t, b_t, tt, t_off):
    _, t_tot, n, c = y.shape
    ob = t_off // tt
    nb = (t_tot - t_off) // tt
    s1 = pl.BlockSpec((tt, n, c), lambda i: (i + ob, 0, 0))
    s2 = pl.BlockSpec((N_DIR, tt, n, c), lambda i: (0, i + ob, 0, 0))
    const = pl.BlockSpec((n, c), lambda i: (0, 0))
    return pl.pallas_call(
        _post_kernel,
        grid=(nb,),
        in_specs=[s2, s1, s1, s1, s2, const, const, const, const],
        out_specs=pl.BlockSpec((tt, n, c), lambda i: (i, 0, 0)),
        out_shape=jax.ShapeDtypeStruct((t_tot - t_off, n, c), F32),
        compiler_params=_cparams(("parallel",)),
        name="post",
    )(y, r, k, v, a, ka_t, rk_t, g_t, b_t)


def _conv_kernel(za_ref, zg_ref, cw_ref, cb_ref, lg_ref, lb_ref, wo_ref, bo_ref, o_ref):
    h = za_ref[...] * _sigmoid(zg_ref[...])
    tm = h.shape[0]
    pos = lax.broadcasted_iota(jnp.int32, (tm, 1), 0) % GRID_W
    half = CONV_WIDTH // 2
    acc = jnp.zeros_like(h)
    for j in range(CONV_WIDTH):
        off = j - half
        sh = pltpu.roll(h, (-off) % tm, 0) if off != 0 else h
        ok = jnp.logical_and(pos + off >= 0, pos + off < GRID_W)
        acc = acc + jnp.where(ok, sh, 0.0) * cw_ref[j:j + 1, :]
    acc = acc + cb_ref[...]
    hn = _ln(acc) * lg_ref[...] + lb_ref[...]
    act = hn * _sigmoid(hn)
    o_ref[...] = jnp.dot(act.astype(BF16), wo_ref[...], preferred_element_type=F32) + bo_ref[...]


def _conv(z, col_a, col_g, cw, cb, lg, lb, wo_bf16, bo, tm):
    n_rows = z.shape[0]
    dc = cw.shape[1]
    dm = wo_bf16.shape[1]
    full = lambda a: pl.BlockSpec(a.shape, lambda m: (0,) * a.ndim)
    cb, lg, lb, bo = cb.reshape(1, dc), lg.reshape(1, dc), lb.reshape(1, dc), bo.reshape(1, dm)
    return pl.pallas_call(
        _conv_kernel,
        grid=(n_rows // tm,),
        in_specs=[pl.BlockSpec((tm, dc), lambda m: (m, col_a)),
                  pl.BlockSpec((tm, dc), lambda m: (m, col_g)),
                  full(cw), full(cb), full(lg), full(lb), full(wo_bf16), full(bo)],
        out_specs=pl.BlockSpec((tm, dm), lambda m: (m, 0)),
        out_shape=jax.ShapeDtypeStruct((n_rows, dm), F32),
        compiler_params=_cparams(("parallel",)),
        name="conv",
    )(z, z, cw, cb, lg, lb, wo_bf16, bo)


def _merge_kernel(o_ref, g_ref, yc_ref, z1_ref, z2_ref, x_ref, gate_ref, wr_ref, wo_ref, bo_ref,
                  lg_ref, lb_ref, out_ref):
    y_rwkv = jnp.dot((o_ref[...] * g_ref[...]).astype(BF16), wr_ref[...], preferred_element_type=F32)
    mix = _sigmoid(z1_ref[...]) * yc_ref[...] + _sigmoid(z2_ref[...]) * y_rwkv
    out = jnp.dot(mix.astype(BF16), wo_ref[...], preferred_element_type=F32) + bo_ref[...]
    h = DEEPNORM_ALPHA * x_ref[...] + gate_ref[0] * out
    out_ref[...] = _ln(h) * lg_ref[...] + lb_ref[...]


def _merge(o_nat, g_nat, y_conv, z, col_g1, x2, mod3, wr_bf16, wo_bf16, bo, lg, lb, tm, rows_per_batch):
    n_rows, dm = x2.shape
    dr = o_nat.shape[1]
    full = lambda a: pl.BlockSpec(a.shape, lambda m: (0,) * a.ndim)
    bo, lg, lb = bo.reshape(1, dm), lg.reshape(1, dm), lb.reshape(1, dm)
    tiles_per_batch = rows_per_batch // tm
    return pl.pallas_call(
        _merge_kernel,
        grid=(n_rows // tm,),
        in_specs=[pl.BlockSpec((tm, dr), lambda m: (m, 0)),
                  pl.BlockSpec((tm, dr), lambda m: (m, 0)),
                  pl.BlockSpec((tm, dm), lambda m: (m, 0)),
                  pl.BlockSpec((tm, dm), lambda m: (m, col_g1)),
                  pl.BlockSpec((tm, dm), lambda m: (m, col_g1 + 1)),
                  pl.BlockSpec((tm, dm), lambda m: (m, 0)),
                  pl.BlockSpec((1, 1, dm), lambda m: ((m // tiles_per_batch) * 6 + 2, 0, 0)),
                  full(wr_bf16), full(wo_bf16), full(bo), full(lg), full(lb)],
        out_specs=pl.BlockSpec((tm, dm), lambda m: (m, 0)),
        out_shape=jax.ShapeDtypeStruct((n_rows, dm), F32),
        compiler_params=_cparams(("parallel",)),
        name="merge",
    )(o_nat, g_nat, y_conv, z, z, x2, mod3, wr_bf16, wo_bf16, bo, lg, lb)


def _router_kernel(x_ref, sh_ref, sc_ref, wr_ref, br_ref, u_ref, idx_ref, gate_ref):
    u = _ln(x_ref[...]) * (1.0 + sc_ref[0]) + sh_ref[0]
    u_ref[...] = u
    logits = jnp.dot(u, wr_ref[...], preferred_element_type=F32, precision=HIGHEST) + br_ref[...]
    lane = lax.broadcasted_iota(jnp.int32, logits.shape, 1)
    vals, idxs = [], []
    l = logits
    for _ in range(TOP_K):
        m = jnp.max(l, axis=-1, keepdims=True)
        ix = jnp.min(jnp.where(l == m, lane, LANES), axis=-1, keepdims=True)
        vals.append(m)
        idxs.append(ix)
        l = jnp.where(lane == ix, -jnp.inf, l)
    es = [jnp.exp(vv - vals[0]) for vv in vals]
    den = es[0] + es[1] + es[2] + es[3]
    idx_out = jnp.zeros(logits.shape, jnp.int32)
    gate_out = jnp.zeros(logits.shape, F32)
    for j in range(TOP_K):
        idx_out = jnp.where(lane == j, idxs[j], idx_out)
        gate_out = jnp.where(lane == j, es[j] / den, gate_out)
    idx_ref[...] = idx_out
    gate_ref[...] = gate_out


def _router(x_mid, mod3, w_router_p, b_router_p, tm, rows_per_batch):
    n_rows, dm = x_mid.shape
    tiles_per_batch = rows_per_batch // tm
    full = lambda a: pl.BlockSpec(a.shape, lambda m: (0,) * a.ndim)
    row = pl.BlockSpec((tm, dm), lambda m: (m, 0))
    lane_out = pl.BlockSpec((tm, LANES), lambda m: (m, 0))
    return pl.pallas_call(
        _router_kernel,
        grid=(n_rows // tm,),
        in_specs=[row,
                  pl.BlockSpec((1, 1, dm), lambda m: ((m // tiles_per_batch) * 6 + 3, 0, 0)),
                  pl.BlockSpec((1, 1, dm), lambda m: ((m // tiles_per_batch) * 6 + 4, 0, 0)),
                  full(w_router_p), full(b_router_p)],
        out_specs=[row, lane_out, lane_out],
        out_shape=[jax.ShapeDtypeStruct((n_rows, dm), F32),
                   jax.ShapeDtypeStruct((n_rows, LANES), jnp.int32),
                   jax.ShapeDtypeStruct((n_rows, LANES), F32)],
        compiler_params=_cparams(("parallel",)),
        name="router",
    )(x_mid, mod3, mod3, w_router_p, b_router_p)


def _group_changed(te_ref, m):
    return jnp.logical_or(m == 0, te_ref[m] != te_ref[jnp.maximum(m - 1, 0)])


def _gmm1_kernel(te_ref, na_ref, x_ref, wg_ref, wu_ref, bg_ref, bu_ref, o_ref, wgb_ref, wub_ref):
    m = pl.program_id(1)

    @pl.when(_group_changed(te_ref, m))
    def _():
        wgb_ref[...] = wg_ref[0].astype(BF16)
        wub_ref[...] = wu_ref[0].astype(BF16)

    @pl.when(m < na_ref[0])
    def _():
        x = x_ref[...]
        gate = jnp.dot(x, wgb_ref[...], preferred_element_type=F32) + bg_ref[0]
        up = jnp.dot(x, wub_ref[...], preferred_element_type=F32) + bu_ref[0]
        gate = jnp.minimum(gate, SWIGLU_LIMIT)
        up = jnp.clip(up, -SWIGLU_LIMIT, SWIGLU_LIMIT)
        hid = (up + 1.0) * (gate * _sigmoid(SWIGLU_ALPHA * gate))
        o_ref[...] = hid.astype(BF16)

    @pl.when(m >= na_ref[0])
    def _():
        o_ref[...] = jnp.zeros_like(o_ref)


def _active_tile(m, na):
    return jnp.minimum(m, na[0] - 1)


def _gmm1(tile_expert, n_active, xs, w_gate_up, b_gate_up3, tm, th):
    p_pad, d = xs.shape
    de = w_gate_up.shape[2] // 2
    nj = de // th
    grid_spec = pltpu.PrefetchScalarGridSpec(
        num_scalar_prefetch=2,
        grid=(nj, p_pad // tm),
        in_specs=[pl.BlockSpec((tm, d), lambda j, m, te, na: (_active_tile(m, na), 0)),
                  pl.BlockSpec((1, d, th), lambda j, m, te, na: (te[m], 0, j)),
                  pl.BlockSpec((1, d, th), lambda j, m, te, na: (te[m], 0, j + nj)),
                  pl.BlockSpec((1, 1, th), lambda j, m, te, na: (te[m], 0, j)),
                  pl.BlockSpec((1, 1, th), lambda j, m, te, na: (te[m], 0, j + nj))],
        out_specs=pl.BlockSpec((tm, th), lambda j, m, te, na: (m, j)),
        scratch_shapes=[pltpu.VMEM((d, th), BF16), pltpu.VMEM((d, th), BF16)],
    )
    return pl.pallas_call(
        _gmm1_kernel,
        grid_spec=grid_spec,
        out_shape=jax.ShapeDtypeStruct((p_pad, de), BF16),
        compiler_params=_cparams(("arbitrary", "arbitrary")),
        name="gmm1",
    )(tile_expert, n_active, xs, w_gate_up, w_gate_up, b_gate_up3, b_gate_up3)


def _gmm2_kernel(te_ref, na_ref, h_ref, w_ref, b_ref, o_ref, wb_ref):
    m = pl.program_id(1)

    @pl.when(_group_changed(te_ref, m))
    def _():
        wb_ref[...] = w_ref[0].astype(BF16)

    @pl.when(m < na_ref[0])
    def _():
        o_ref[...] = jnp.dot(h_ref[...], wb_ref[...], preferred_element_type=F32) + b_ref[0]

    @pl.when(m >= na_ref[0])
    def _():
        o_ref[...] = jnp.zeros_like(o_ref)


def _gmm2(tile_expert, n_active, hid, w_down, b_down3, tm, tn):
    p_pad, de = hid.shape
    d = w_down.shape[2]
    grid_spec = pltpu.PrefetchScalarGridSpec(
        num_scalar_prefetch=2,
        grid=(d // tn, p_pad // tm),
        in_specs=[pl.BlockSpec((tm, de), lambda j, m, te, na: (_active_tile(m, na), 0)),
                  pl.BlockSpec((1, de, tn), lambda j, m, te, na: (te[m], 0, j)),
                  pl.BlockSpec((1, 1, tn), lambda j, m, te, na: (te[m], 0, j))],
        out_specs=pl.BlockSpec((tm, tn), lambda j, m, te, na: (m, j)),
        scratch_shapes=[pltpu.VMEM((de, tn), BF16)],
    )
    return pl.pallas_call(
        _gmm2_kernel,
        grid_spec=grid_spec,
        out_shape=jax.ShapeDtypeStruct((p_pad, d), F32),
        compiler_params=_cparams(("arbitrary", "arbitrary")),
        name="gmm2",
    )(tile_expert, n_active, hid, w_down, b_down3)


def _row_copy(src_hbm, src_row, buf_ref, slot, dst_row, sem_ref):
    return pltpu.make_async_copy(src_hbm.at[pl.ds(src_row, 1), :],
                                 buf_ref.at[slot, pl.ds(dst_row, 1), :], sem_ref.at[slot])


def _wait_rows(src_hbm, buf_ref, slot, sem_ref):
    n = buf_ref.shape[1]
    pltpu.make_async_copy(src_hbm.at[pl.ds(0, n), :], buf_ref.at[slot], sem_ref.at[slot]).wait()


def _dispatch_kernel(rt_ref, na_ref, u_hbm, o_ref, buf_ref, sem_ref, *, tm):
    i = pl.program_id(0)
    na = na_ref[0]

    def issue(tile, slot):
        def body(r, carry):
            _row_copy(u_hbm, rt_ref[tile * tm + r], buf_ref, slot, r, sem_ref).start()
            return carry
        lax.fori_loop(0, tm, body, 0, unroll=ISSUE_UNROLL)

    @pl.when(jnp.logical_and(i == 0, na > 0))
    def _():
        issue(0, 0)

    @pl.when(i + 1 < na)
    def _():
        issue(i + 1, (i + 1) % 2)

    @pl.when(i < na)
    def _():
        slot = i % 2
        _wait_rows(u_hbm, buf_ref, slot, sem_ref)
        o_ref[...] = buf_ref[slot].astype(BF16)

    @pl.when(i >= na)
    def _():
        o_ref[...] = jnp.zeros_like(o_ref)


def _dispatch(row_token, n_active, u_f, tm):
    p_pad = row_token.shape[0]
    dm = u_f.shape[1]
    grid_spec = pltpu.PrefetchScalarGridSpec(
        num_scalar_prefetch=2,
        grid=(p_pad // tm,),
        in_specs=[pl.BlockSpec(memory_space=pl.ANY)],
        out_specs=pl.BlockSpec((tm, dm), lambda i, rt, na: (i, 0)),
        scratch_shapes=[pltpu.VMEM((2, tm, dm), F32), pltpu.SemaphoreType.DMA((2,))],
    )
    return pl.pallas_call(
        functools.partial(_dispatch_kernel, tm=tm),
        grid_spec=grid_spec,
        out_shape=jax.ShapeDtypeStruct((p_pad, dm), BF16),
        compiler_params=_cparams(("arbitrary",)),
        name="dispatch",
    )(row_token, n_active, u_f)


def _combine_kernel(pos_ref, ys_hbm, x_ref, gl_ref, gate_ref, lg_ref, lb_ref, o_ref, buf_ref, sem_ref, *, tm):
    i = pl.program_id(0)
    n = pl.num_programs(0)

    def issue(tile, slot):
        def body(r, carry):
            for j in range(TOP_K):
                p = pos_ref[(tile * tm + r) * TOP_K + j]
                _row_copy(ys_hbm, p, buf_ref, slot, j * tm + r, sem_ref).start()
            return carry
        lax.fori_loop(0, tm, body, 0, unroll=ISSUE_UNROLL // TOP_K)

    @pl.when(i == 0)
    def _():
        issue(0, 0)

    @pl.when(i + 1 < n)
    def _():
        issue(i + 1, (i + 1) % 2)

    slot = i % 2
    _wait_rows(ys_hbm, buf_ref, slot, sem_ref)
    gl = gl_ref[...]
    moe = gl[:, 0:1] * buf_ref[slot, pl.ds(0, tm), :]
    for j in range(1, TOP_K):
        moe = moe + gl[:, j:j + 1] * buf_ref[slot, pl.ds(j * tm, tm), :]
    h = DEEPNORM_ALPHA * x_ref[...] + gate_ref[0] * moe
    o_ref[...] = _ln(h) * lg_ref[...] + lb_ref[...]


def _combine(pos, ys, x_mid, gate_l, mod3, lg, lb, tm, rows_per_batch):
    n_rows, dm = x_mid.shape
    tiles_per_batch = rows_per_batch // tm
    row = pl.BlockSpec((tm, dm), lambda m, p: (m, 0))
    vec = pl.BlockSpec((1, dm), lambda m, p: (0, 0))
    grid_spec = pltpu.PrefetchScalarGridSpec(
        num_scalar_prefetch=1,
        grid=(n_rows // tm,),
        in_specs=[pl.BlockSpec(memory_space=pl.ANY), row,
                  pl.BlockSpec((tm, LANES), lambda m, p: (m, 0)),
                  pl.BlockSpec((1, 1, dm), lambda m, p: ((m // tiles_per_batch) * 6 + 5, 0, 0)),
                  vec, vec],
        out_specs=row,
        scratch_shapes=[pltpu.VMEM((2, TOP_K * tm, dm), F32), pltpu.SemaphoreType.DMA((2,))],
    )
    return pl.pallas_call(
        functools.partial(_combine_kernel, tm=tm),
        grid_spec=grid_spec,
        out_shape=jax.ShapeDtypeStruct((n_rows, dm), F32),
        compiler_params=_cparams(("arbitrary",)),
        name="combine",
    )(pos, ys, x_mid, gate_l, mod3, lg.reshape(1, dm), lb.reshape(1, dm))


def _to_scan_layout(a, n_heads):
    *lead, b, t, _ = a.shape
    nl = len(lead)
    a = a.reshape(*lead, b, t, n_heads, HEAD_SIZE)
    perm = tuple(range(nl)) + (nl + 1, nl + 3, nl + 0, nl + 2)
    return a.transpose(perm).reshape(*lead, t, HEAD_SIZE, b * n_heads)


def _head_tile(p, n_heads, batch):
    return jnp.tile(p.reshape(n_heads, HEAD_SIZE).T, (1, batch))


def _pick(n, pref):
    t = min(n, pref)
    while n % t:
        t //= 2
    return t


def kernel(x, c, ctx, c_ctx, w_ada, b_ada, w_in, b_in, shift_mu, conv_w, conv_b, conv_ln_g, conv_ln_b,
           w_conv_o, b_conv_o, w0, w2, a0, a2, g2, k_k, k_a, r_k, lnx_g, lnx_b, w_rwkv_o, w_out, b_out,
           ln1_g, ln1_b, w_router, b_router, w_gate_up, b_gate_up, w_down, b_down, ln2_g, ln2_b):
    bsz, seq, dm = x.shape
    n_ctx = ctx.shape[1]
    dc = conv_w.shape[2]
    dr = k_k.shape[1]
    n_heads = dr // HEAD_SIZE
    d_shift = shift_mu.shape[2]
    n_exp = w_router.shape[2]
    l = 0
    n_tok = bsz * seq

    c_rows = jnp.zeros((16, dm), F32).at[:bsz].set(c).at[bsz].set(c_ctx)
    mod = _ada(c_rows, w_ada[l], b_ada[l])
    mod3 = mod.reshape(16 * 6, 1, dm)

    pad = ZR_PAD - d_shift
    wl = w_in[l]
    w_p = jnp.concatenate([wl[:, 2 * dc:2 * dc + d_shift], jnp.zeros((dm, pad), F32),
                           wl[:, :2 * dc], wl[:, 2 * dc + d_shift:]], axis=1).astype(BF16)
    bl = b_in[l]
    b_p = jnp.concatenate([bl[2 * dc:2 * dc + d_shift], jnp.zeros((pad,), F32),
                           bl[:2 * dc], bl[2 * dc + d_shift:]])
    x2 = x.reshape(n_tok, dm)
    tm_in = _pick(seq, 1024)
    z_l = _lnmod_mm(x2, mod3, lambda m: (m * tm_in) // seq, 0, w_p, b_p, tm_in, 1024)
    tm_c = _pick(bsz * n_ctx, 1024)
    z_c = _lnmod_mm(ctx.reshape(bsz * n_ctx, dm), mod3, lambda m: bsz, 0,
                    w_p[:, :ZR_PAD], b_p[:ZR_PAD], tm_c, 1024)

    mu_p = jnp.pad(shift_mu[l], ((0, 0), (0, pad)))
    w2blk = jnp.zeros((LANES, N_DIR * dr), F32)
    a2blk = jnp.zeros((LANES, N_DIR * dr), F32)
    for d in range(N_DIR):
        w2blk = w2blk.at[d * D_LORA:(d + 1) * D_LORA, d * dr:(d + 1) * dr].set(w2[l, d])
        a2blk = a2blk.at[d * D_LORA:(d + 1) * D_LORA, d * dr:(d + 1) * dr].set(a2[l, d])
    g2p = jnp.zeros((2 * LANES, dr), F32).at[:D_GATE_LORA].set(g2[l])
    prep_args = (mu_p, w2blk.astype(BF16), a2blk.astype(BF16), g2p.astype(BF16), w0[l], a0[l])
    tt_p = _pick(seq, 256)
    r_l, k_l, v_l, a_l, w_l, g_l = _prep(z_l.reshape(bsz, seq, -1), *prep_args, tt_p, dr)
    r_c, k_c, v_c, a_c, w_c, _ = _prep(z_c.reshape(bsz, n_ctx, -1), *prep_args, _pick(n_ctx, 256), dr)

    ctx_ops = tuple(_to_scan_layout(t, n_heads) for t in (r_c, k_c, v_c, a_c, w_c))
    lat_ops = tuple(_to_scan_layout(t, n_heads) for t in (r_l, k_l, v_l, a_l, w_l))
    ka_t = _head_tile(k_a[l], n_heads, bsz)
    tt_s = _pick(n_ctx, 32)
    y_s = _scan(ctx_ops, lat_ops, _head_tile(k_k[l], n_heads, bsz), ka_t, tt_s)
    o_s = _post(y_s, *lat_ops[:4], ka_t, jnp.tile(r_k[l].T, (1, bsz)),
                _head_tile(lnx_g[l], n_heads, bsz), _head_tile(lnx_b[l], n_heads, bsz), tt_s, 0)
    o_nat = o_s.reshape(seq, HEAD_SIZE, bsz, n_heads).transpose(2, 0, 3, 1).reshape(n_tok, dr)

    zcol = ZR_PAD // dc
    y_conv = _conv(z_l, zcol, zcol + 1, conv_w[l], conv_b[l], conv_ln_g[l], conv_ln_b[l],
                   w_conv_o[l].astype(BF16), b_conv_o[l], _pick(seq, 512))
    gcol = (ZR_PAD + 2 * dc) // dm
    tm_m = _pick(seq, 128)
    x_mid = _merge(o_nat, g_l.reshape(n_tok, dr), y_conv, z_l, gcol, x2, mod3, w_rwkv_o[l].astype(BF16),
                   w_out[l].astype(BF16), b_out[l], ln1_g[l], ln1_b[l], tm_m, seq)

    w_r_p = jnp.zeros((dm, LANES), F32).at[:, :n_exp].set(w_router[l])
    b_r_p = jnp.full((1, LANES), -1e30, F32).at[0, :n_exp].set(b_router[l])
    u_f, idx_l, gate_l = _router(x_mid, mod3, w_r_p, b_r_p, _pick(seq, 256), seq)
    top_idx = idx_l[:, :TOP_K]

    tm_e = EXPERT_ROW_TILE
    n_pair = n_tok * TOP_K
    n_tiles = -(-n_pair // tm_e) + n_exp
    p_pad = n_tiles * tm_e
    eq = top_idx[:, :, None] == jnp.arange(n_exp, dtype=jnp.int32)[None, None, :]
    onehot = jnp.sum(eq, axis=1, dtype=jnp.int32)
    cnt_incl = jnp.cumsum(onehot, axis=0)
    counts = cnt_incl[-1]
    cnt_excl = cnt_incl - onehot
    padded = ((counts + tm_e - 1) // tm_e) * tm_e
    pend = jnp.cumsum(padded)
    pstart = pend - padded
    pos = jnp.sum(jnp.where(eq, (cnt_excl + pstart[None, :])[:, None, :], 0), axis=2).reshape(-1)
    n_active = (pend[-1] // tm_e).astype(jnp.int32).reshape(1)
    tile_start = jnp.arange(n_tiles, dtype=jnp.int32) * tm_e
    tile_expert = jnp.sum(pend[None, :] <= tile_start[:, None], axis=1, dtype=jnp.int32)
    last_e = jnp.sum(pend <= pend[-1] - 1, dtype=jnp.int32)
    tile_expert = jnp.minimum(tile_expert, last_e)
    keys = (top_idx * n_tok + jnp.arange(n_tok, dtype=jnp.int32)[:, None]).reshape(-1)
    sorted_tok = jnp.sort(keys) % n_tok
    start = jnp.cumsum(counts) - counts
    q = (tile_start - pstart[tile_expert])[:, None] + jnp.arange(tm_e, dtype=jnp.int32)[None, :]
    src = jnp.clip(start[tile_expert][:, None] + q, 0, n_pair - 1)
    row_token = jnp.where(q < counts[tile_expert][:, None], sorted_tok[src], 0).reshape(-1)

    xs = _dispatch(row_token, n_active, u_f, tm_e)
    hid = _gmm1(tile_expert, n_active, xs, w_gate_up[l], b_gate_up[l].reshape(n_exp, 1, -1), tm_e, 1024)
    ys = _gmm2(tile_expert, n_active, hid, w_down[l], b_down[l].reshape(n_exp, 1, -1), tm_e, 1024)
    out = _combine(pos, ys, x_mid, gate_l, mod3, ln2_g[l], ln2_b[l], _pick(seq, 256), seq)
    return out.reshape(bsz, seq, dm)
```
